```python
import math
import jax, jax.numpy as jnp
from jax import lax
import numpy as np

D_MODEL = 1024
BATCH = 8
SEQ = 8192
DEPTH = 2

GRID_W = 64
CTX_LEN = 256
N_HEADS = 16
HEAD_DIM = D_MODEL // N_HEADS
WIN_R_MAX = 8
WIN_C = 16
CONV_W = 3
N_MIXERS = 2
N_CONV_LAYERS = (DEPTH + 1) // 2
N_NA_LAYERS = DEPTH // 2
EPS = 1e-6

kernel_name = "hybrid_conv_natten_prefix_dit"


def rms_norm(x, g):
    xf = x.astype(jnp.float32)
    y = xf * lax.rsqrt(jnp.mean(xf * xf, axis=-1, keepdims=True) + EPS)
    return y.astype(x.dtype) * g


def modulation(cvec, w, b):
    m = jax.nn.silu(cvec) @ w + b
    return jnp.split(m, 3, axis=-1)


def dwconv3_centred(u, w):
    up = jnp.pad(u, ((0, 0), (1, 1), (0, 0)))
    return w[0] * up[:, :-2] + w[1] * up[:, 1:-1] + w[2] * up[:, 2:]


def conv_mixer(h, w_in, conv_w, w_out):
    b_g, c_g, u, z = jnp.split(h @ w_in, 4, axis=-1)
    y = b_g * dwconv3_centred(c_g * u, conv_w) * jax.nn.silu(z)
    return y @ w_out


def dense_attention(q, k, v):
    s = jnp.einsum('bqhd,bkhd->bhqk', q, k) * (q.shape[-1] ** -0.5)
    p = jax.nn.softmax(s.astype(jnp.float32), axis=-1).astype(v.dtype)
    return jnp.einsum('bhqk,bkhd->bqhd', p, v)


def neighbourhood_attention(q, k, v, k_ctx, v_ctx, rpb):
    B, L, H, Dh = q.shape
    rows = L // GRID_W
    win_r = min(WIN_R_MAX, rows)
    n_lat = win_r * WIN_C
    qg = (q * (Dh ** -0.5)).reshape(B, rows, GRID_W, H, Dh)
    kg = k.reshape(B, rows, GRID_W, H, Dh)
    vg = v.reshape(B, rows, GRID_W, H, Dh)
    cols = jnp.arange(GRID_W)
    col_start = jnp.clip(cols - WIN_C // 2, 0, GRID_W - WIN_C)
    col_idx = col_start[:, None] + jnp.arange(WIN_C)[None, :]
    dc_idx = col_idx - cols[:, None] + (WIN_C - 1)

    def row_step(args):
        r, q_row = args
        rs = jnp.clip(r - win_r // 2, 0, rows - win_r)
        k_band = lax.dynamic_slice_in_dim(kg, rs, win_r, axis=1)
        v_band = lax.dynamic_slice_in_dim(vg, rs, win_r, axis=1)
        k_win = jnp.take(k_band, col_idx, axis=2)
        v_win = jnp.take(v_band, col_idx, axis=2)
        dr_idx = rs + jnp.arange(win_r) - r + (WIN_R_MAX - 1)
        bias = rpb[:, dr_idx[:, None, None], dc_idx[None, :, :]]
        s_lat = jnp.einsum('bchd,bvcwhd->bhcvw', q_row, k_win) + jnp.transpose(bias, (0, 2, 1, 3))[None]
        s_ctx = jnp.einsum('bchd,bjhd->bhcj', q_row, k_ctx)
        logits = jnp.concatenate([s_lat.reshape(B, H, GRID_W, n_lat), s_ctx], axis=-1)
        p = jax.nn.softmax(logits.astype(jnp.float32), axis=-1).astype(v.dtype)
        p_lat = p[..., :n_lat].reshape(B, H, GRID_W, win_r, WIN_C)
        p_ctx = p[..., n_lat:]
        return (jnp.einsum('bhcvw,bvcwhd->bchd', p_lat, v_win)
                + jnp.einsum('bhcj,bjhd->bchd', p_ctx, v_ctx))

    out = lax.map(row_step, (jnp.arange(rows), jnp.moveaxis(qg, 1, 0)))
    return jnp.moveaxis(out, 0, 1).reshape(B, L, H, Dh)


def setup_inputs(seed: int = 0) -> dict:
    key = jax.random.key(seed)
    ks = jax.random.split(key, 16)
    D = D_MODEL
    s = D ** -0.5
    return {
        "x": jax.random.normal(ks[0], (BATCH, SEQ, D), jnp.float32),
        "c": jax.random.normal(ks[1], (BATCH, D), jnp.float32),
        "ctx": jax.random.normal(ks[2], (BATCH, CTX_LEN, D), jnp.float32),
        "c_ctx": jax.random.normal(ks[3], (D,), jnp.float32),
        "g_pre": 1.0 + 0.1 * jax.random.normal(ks[4], (DEPTH, D), jnp.float32),
        "g_post": 1.0 + 0.1 * jax.random.normal(ks[5], (DEPTH, D), jnp.float32),
        "w_mod": 0.5 * s * jax.random.normal(ks[6], (DEPTH, D, 3 * D), jnp.float32),
        "b_mod": 0.02 * jax.random.normal(ks[7], (DEPTH, 3 * D), jnp.float32),
        "w_in_conv": s * jax.random.normal(ks[8], (N_CONV_LAYERS, D, 4 * D), jnp.float32),
        "conv_w": (CONV_W ** -0.5) * jax.random.normal(ks[9], (N_CONV_LAYERS, CONV_W, D), jnp.float32),
        "w_out_conv": s * jax.random.normal(ks[10], (N_CONV_LAYERS, D, D), jnp.float32),
        "w_in_na": s * jax.random.normal(ks[11], (N_NA_LAYERS, D, 4 * D), jnp.float32),
        "rpb": 0.1 * jax.random.normal(ks[12], (N_NA_LAYERS, N_HEADS, 2 * WIN_R_MAX - 1, 2 * WIN_C - 1), jnp.float32),
        "w_out_na": s * jax.random.normal(ks[13], (N_NA_LAYERS, D, D), jnp.float32),
    }


def reference(x, c, ctx, c_ctx, g_pre, g_post, w_mod, b_mod,
              w_in_conv, conv_w, w_out_conv, w_in_na, rpb, w_out_na):
    B, L, D = x.shape
    Bc, Lc, _ = ctx.shape
    for i in range(DEPTH):
        last = i == DEPTH - 1
        j = i // N_MIXERS
        sh, sc, gt = modulation(c, w_mod[i], b_mod[i])
        sh_c, sc_c, gt_c = modulation(c_ctx, w_mod[i], b_mod[i])
        h = rms_norm(x, g_pre[i]) * (1 + sc[:, None, :]) + sh[:, None, :]
        hc = rms_norm(ctx, g_pre[i]) * (1 + sc_c) + sh_c
        if i % N_MIXERS == 0:
            y = conv_mixer(h, w_in_conv[j], conv_w[j], w_out_conv[j])
            if not last:
                yc = conv_mixer(hc, w_in_conv[j], conv_w[j], w_out_conv[j])
        else:
            wi = w_in_na[j]
            q, k, v, z = jnp.split(h @ wi, 4, axis=-1)
            k_c, v_c = jnp.split(hc @ wi[:, D:3 * D], 2, axis=-1)
            heads = lambda t, n: t.reshape(B, n, N_HEADS, HEAD_DIM)
            k_c, v_c = heads(k_c, Lc), heads(v_c, Lc)
            o = neighbourhood_attention(heads(q, L), heads(k, L), heads(v, L), k_c, v_c, rpb[j])
            y = (jax.nn.silu(z) * o.reshape(B, L, D)) @ w_out_na[j]
            if not last:
                q_c = heads(hc @ wi[:, :D], Lc)
                z_c = hc @ wi[:, 3 * D:]
                oc = dense_attention(q_c, k_c, v_c)
                yc = (jax.nn.silu(z_c) * oc.reshape(Bc, Lc, D)) @ w_out_na[j]
        x = x + gt[:, None, :] * rms_norm(y, g_post[i])
        if not last:
            ctx = ctx + gt_c * rms_norm(yc, g_post[i])
    return x
```

```python
import functools

import jax
import jax.numpy as jnp
from jax import lax
from jax.experimental import pallas as pl
from jax.experimental.pallas import tpu as pltpu

F32 = jnp.float32
BF16 = jnp.bfloat16

GRID_W = 64
N_HEADS = 16
HEAD_DIM = 64
WIN_R = 8
WIN_C = 16
EPS = 1e-6
NEG = -1e30

LANES = 128
HEADS_PER_GROUP = LANES // HEAD_DIM
HALO = 8
Q_ROWS = 4
Q_TOK = Q_ROWS * GRID_W
N_KEY_TILES = 3
GROUPS_PER_STEP = 4
VMEM_LIMIT = 56 * 1024 * 1024


def _params(n_axes):
    return pltpu.CompilerParams(dimension_semantics=("arbitrary",) * n_axes,
                                vmem_limit_bytes=VMEM_LIMIT)


def _silu(v):
    return v * jax.nn.sigmoid(v)


def _rms(v):
    return v * lax.rsqrt(jnp.mean(v * v, axis=-1, keepdims=True) + EPS)


def _dot(a, b):
    return jnp.dot(a, b, preferred_element_type=F32)


def _mod_kernel(c_ref, w_ref, b_ref, o_ref):
    s = _silu(c_ref[...])
    o_ref[0] = jnp.dot(s, w_ref[0], preferred_element_type=F32,
                       precision=lax.Precision.HIGHEST) + b_ref[0]


def _mod_call(cvec, w_mod, b_mod):
    depth, d, d3 = w_mod.shape
    rows = cvec.shape[0]
    tn = 1024
    return pl.pallas_call(
        _mod_kernel,
        grid=(depth, d3 // tn),
        in_specs=[
            pl.BlockSpec((rows, d), lambda i, n: (0, 0)),
            pl.BlockSpec((1, d, tn), lambda i, n: (i, 0, n)),
            pl.BlockSpec((1, 1, tn), lambda i, n: (i, 0, n)),
        ],
        out_specs=pl.BlockSpec((1, rows, tn), lambda i, n: (i, 0, n)),
        out_shape=jax.ShapeDtypeStruct((depth, rows, d3), F32),
        compiler_params=_params(2),
        name="mod",
    )(cvec, w_mod, b_mod.reshape(depth, 1, d3))


def _norm_mod(xv, g, mod, d):
    sh = mod[:, :d]
    sc = mod[:, d:2 * d]
    return _rms(xv) * g * (1.0 + sc) + sh


def _conv_kernel(x_ref, xp_ref, xn_ref, mod_ref, gpre_ref, gpost_ref, win_ref, cw_ref, wout_ref,
                 o_ref, yv_ref, *, tiles_per_seq, cb):
    tm, d = x_ref.shape
    ti = pl.program_id(0) % tiles_per_seq
    mod = mod_ref[0]
    g = gpre_ref[...]
    xm = x_ref[...]
    hm = _norm_mod(xm, g, mod, d)
    keep_p = (ti > 0).astype(F32)
    keep_n = (ti < tiles_per_seq - 1).astype(F32)
    hp = _norm_mod(xp_ref[...], g, mod, d)
    hn = _norm_mod(xn_ref[...], g, mod, d)
    hcat = jnp.concatenate([hp, hm, hn], axis=0).astype(BF16)
    hmb = hm.astype(BF16)
    rid = lax.broadcasted_iota(jnp.int32, (tm + 2 * HALO, 1), 0)
    rowkeep = jnp.where(rid < HALO, keep_p, jnp.where(rid >= tm + HALO, keep_n, 1.0))
    for nb in range(d // cb):
        c0 = nb * cb
        cg = _dot(hcat, win_ref[:, d + c0:d + c0 + cb])
        u = _dot(hcat, win_ref[:, 2 * d + c0:2 * d + c0 + cb])
        cu = cg * u * rowkeep
        w = cw_ref[:, c0:c0 + cb]
        conv = (w[0:1] * cu[HALO - 1:HALO - 1 + tm]
                + w[1:2] * cu[HALO:HALO + tm]
                + w[2:3] * cu[HALO + 1:HALO + 1 + tm])
        bg = _dot(hmb, win_ref[:, c0:c0 + cb])
        z = _dot(hmb, win_ref[:, 3 * d + c0:3 * d + c0 + cb])
        yv_ref[:, c0:c0 + cb] = (bg * conv * _silu(z)).astype(BF16)
    y = _dot(yv_ref[...], wout_ref[...])
    gt = mod[:, 2 * d:]
    o_ref[...] = xm + gt * (_rms(y) * gpost_ref[...])


def _conv_call(x2, mod, g_pre, g_post, w_in, conv_w, w_out, *, seq_len, tm, mod_per_seq):
    n_tok, d = x2.shape
    tiles_per_seq = seq_len // tm
    n_tiles = n_tok // tm
    hb = tm // HALO
    last_hb = n_tok // HALO - 1
    if mod_per_seq:
        mod_map = lambda i: (i // tiles_per_seq, 0, 0)
    else:
        mod_map = lambda i: (0, 0, 0)
    const = lambda i: (0, 0)
    return pl.pallas_call(
        functools.partial(_conv_kernel, tiles_per_seq=tiles_per_seq, cb=256),
        grid=(n_tiles,),
        in_specs=[
            pl.BlockSpec((tm, d), lambda i: (i, 0)),
            pl.BlockSpec((HALO, d), lambda i: (jnp.maximum(i * hb - 1, 0), 0)),
            pl.BlockSpec((HALO, d), lambda i: (jnp.minimum((i + 1) * hb, last_hb), 0)),
            pl.BlockSpec((1, 1, 3 * d), mod_map),
            pl.BlockSpec((1, d), const),
            pl.BlockSpec((1, d), const),
            pl.BlockSpec((d, 4 * d), const),
            pl.BlockSpec((3, d), const),
            pl.BlockSpec((d, d), const),
        ],
        out_specs=pl.BlockSpec((tm, d), lambda i: (i, 0)),
        out_shape=jax.ShapeDtypeStruct((n_tok, d), F32),
        scratch_shapes=[pltpu.VMEM((tm, d), BF16)],
        compiler_params=_params(1),
        name="conv_layer",
    )(x2, x2, x2, mod, g_pre, g_post, w_in, conv_w, w_out)


def _proj_kernel(x_ref, mod_ref, gpre_ref, wq_ref, wkt_ref, wv_ref, wz_ref,
                 q_ref, kt_ref, v_ref, z_ref):
    tm, d = x_ref.shape
    n_groups = d // LANES
    hb = _norm_mod(x_ref[...], gpre_ref[...], mod_ref[0], d).astype(BF16)
    q = _dot(hb, wq_ref[...]) * (HEAD_DIM ** -0.5)
    for p in range(n_groups):
        q_ref[0, p] = q[:, p * LANES:(p + 1) * LANES].astype(BF16)
    v = _dot(hb, wv_ref[...])
    for p in range(n_groups):
        v_ref[0, p] = v[:, p * LANES:(p + 1) * LANES].astype(BF16)
    z_ref[...] = _dot(hb, wz_ref[...]).astype(BF16)
    kt = lax.dot_general(wkt_ref[...], hb, (((1,), (1,)), ((), ())), preferred_element_type=F32)
    kt_ref[0] = kt.astype(BF16)


def _proj_call(x2, mod, g_pre, wq, wkt, wv, wz, *, n_seq, seq_len, tm, mod_per_seq):
    n_tok, d = x2.shape
    tps = seq_len // tm
    n_groups = d // LANES
    if mod_per_seq:
        mod_map = lambda b, t: (b, 0, 0)
    else:
        mod_map = lambda b, t: (0, 0, 0)
    const = lambda b, t: (0, 0)
    grouped = jax.ShapeDtypeStruct((n_seq, n_groups, seq_len, LANES), BF16)
    return pl.pallas_call(
        _proj_kernel,
        grid=(n_seq, tps),
        in_specs=[
            pl.BlockSpec((tm, d), lambda b, t: (b * tps + t, 0)),
            pl.BlockSpec((1, 1, 3 * d), mod_map),
            pl.BlockSpec((1, d), const),
            pl.BlockSpec((d, d), const),
            pl.BlockSpec((d, d), const),
            pl.BlockSpec((d, d), const),
            pl.BlockSpec((d, d), const),
        ],
        out_specs=[
            pl.BlockSpec((1, n_groups, tm, LANES), lambda b, t: (b, 0, t, 0)),
            pl.BlockSpec((1, d, tm), lambda b, t: (b, 0, t)),
            pl.BlockSpec((1, n_groups, tm, LANES), lambda b, t: (b, 0, t, 0)),
            pl.BlockSpec((tm, d), lambda b, t: (b * tps + t, 0)),
        ],
        out_shape=[
            grouped,
            jax.ShapeDtypeStruct((n_seq, d, seq_len), BF16),
            grouped,
            jax.ShapeDtypeStruct((n_tok, d), BF16),
        ],
        compiler_params=_params(2),
        name="na_proj",
    )(x2, mod, g_pre, wq, wkt, wv, wz)


def _bias_tables(rpb):
    h = rpb.shape[0]
    c = jnp.arange(GRID_W)
    cs = jnp.clip(c - WIN_C // 2, 0, GRID_W - WIN_C)
    kc = jnp.arange(GRID_W)
    col_ok = (kc[None, :] >= cs[:, None]) & (kc[None, :] < cs[:, None] + WIN_C)
    dc = jnp.clip(kc[None, :] - c[:, None] + (WIN_C - 1), 0, 2 * WIN_C - 2)
    cb = jnp.where(col_ok[None, None], rpb[:, :, dc], NEG)
    qr = jnp.arange(Q_ROWS)[:, None]
    kv = jnp.arange(N_KEY_TILES * Q_ROWS)[None, :]
    dr = jnp.clip(kv - qr + (WIN_R - 1 - Q_ROWS), 0, 2 * WIN_R - 2)
    ok = jnp.stack([
        jnp.broadcast_to(kv >= Q_ROWS, dr.shape),
        (kv >= qr) & (kv < qr + WIN_R),
        jnp.broadcast_to(kv < WIN_R, dr.shape),
    ])
    t = cb[:, dr]
    t = jnp.where(ok[:, None, :, :, None, None], t[None], NEG)
    t = jnp.transpose(t, (0, 1, 2, 4, 3, 5))
    return t.reshape(3, h // HEADS_PER_GROUP, HEADS_PER_GROUP, Q_TOK, N_KEY_TILES * Q_TOK)


def _attn_kernel(q_ref, ktp_ref, ktc_ref, ktn_ref, vp_ref, vc_ref, vn_ref, kct_ref, vctx_ref,
                 bias_ref, o_ref, *, steps_per_seq):
    s = pl.program_id(2)
    lane = lax.broadcasted_iota(jnp.int32, (Q_TOK, LANES), 1)
    kct = kct_ref[0]
    vctx = vctx_ref[0, 0]
    for gi in range(GROUPS_PER_STEP):
        kts, vs = [], []
        for j in (gi - 1, gi, gi + 1):
            if j < 0:
                kts.append(ktp_ref[0])
                vs.append(vp_ref[0, 0])
            elif j >= GROUPS_PER_STEP:
                kts.append(ktn_ref[0])
                vs.append(vn_ref[0, 0])
            else:
                kts.append(ktc_ref[0, :, j * Q_TOK:(j + 1) * Q_TOK])
                vs.append(vc_ref[0, 0, j * Q_TOK:(j + 1) * Q_TOK, :])
        if gi == 0:
            pat = jnp.where(s == 0, 0, 1)
        elif gi == GROUPS_PER_STEP - 1:
            pat = jnp.where(s == steps_per_seq - 1, 2, 1)
        else:
            pat = 1
        qg = q_ref[0, 0, gi * Q_TOK:(gi + 1) * Q_TOK, :]
        outs = []
        for a in range(HEADS_PER_GROUP):
            in_head = (lane >= a * HEAD_DIM) & (lane < (a + 1) * HEAD_DIM)
            qa = jnp.where(in_head, qg, jnp.zeros_like(qg))
            sc = [_dot(qa, kts[j]) + bias_ref[pat, 0, a, :, j * Q_TOK:(j + 1) * Q_TOK]
                  for j in range(N_KEY_TILES)]
            sc.append(_dot(qa, kct))
            m = jnp.maximum(jnp.maximum(sc[0], sc[1]), jnp.maximum(sc[2], sc[3]))
            m = jnp.max(m, axis=-1, keepdims=True)
            ps = [jnp.exp(t - m) for t in sc]
            l = jnp.sum(ps[0] + ps[1] + ps[2] + ps[3], axis=-1, keepdims=True)
            acc = _dot(ps[3].astype(BF16), vctx)
            for j in range(N_KEY_TILES):
                acc = acc + _dot(ps[j].astype(BF16), vs[j])
            outs.append(acc / l)
        o = jnp.where(lane < HEAD_DIM, outs[0], outs[1])
        o_ref[0, 0, gi * Q_TOK:(gi + 1) * Q_TOK, :] = o.astype(BF16)


def _attn_call(q, kt, v, kct, vctx, bias):
    n_seq, n_groups, seq_len, _ = q.shape
    ctx_len = kct.shape[-1]
    step_tok = GROUPS_PER_STEP * Q_TOK
    steps = seq_len // step_tok
    n_tiles = seq_len // Q_TOK
    prev_t = lambda s: jnp.maximum(s * GROUPS_PER_STEP - 1, 0)
    next_t = lambda s: jnp.minimum((s + 1) * GROUPS_PER_STEP, n_tiles - 1)
    return pl.pallas_call(
        functools.partial(_attn_kernel, steps_per_seq=steps),
        grid=(n_groups, n_seq, steps),
        in_specs=[
            pl.BlockSpec((1, 1, step_tok, LANES), lambda p, b, s: (b, p, s, 0)),
            pl.BlockSpec((1, LANES, Q_TOK), lambda p, b, s: (b, p, prev_t(s))),
            pl.BlockSpec((1, LANES, step_tok), lambda p, b, s: (b, p, s)),
            pl.BlockSpec((1, LANES, Q_TOK), lambda p, b, s: (b, p, next_t(s))),
            pl.BlockSpec((1, 1, Q_TOK, LANES), lambda p, b, s: (b, p, prev_t(s), 0)),
            pl.BlockSpec((1, 1, step_tok, LANES), lambda p, b, s: (b, p, s, 0)),
            pl.BlockSpec((1, 1, Q_TOK, LANES), lambda p, b, s: (b, p, next_t(s), 0)),
            pl.BlockSpec((1, LANES, ctx_len), lambda p, b, s: (b, p, 0)),
            pl.BlockSpec((1, 1, ctx_len, LANES), lambda p, b, s: (b, p, 0, 0)),
            pl.BlockSpec((3, 1, HEADS_PER_GROUP, Q_TOK, N_KEY_TILES * Q_TOK),
                         lambda p, b, s: (0, p, 0, 0, 0)),
        ],
        out_specs=pl.BlockSpec((1, 1, step_tok, LANES), lambda p, b, s: (b, p, s, 0)),
        out_shape=jax.ShapeDtypeStruct(q.shape, BF16),
        compiler_params=_params(3),
        name="na_attn",
    )(q, kt, kt, kt, v, v, v, kct, vctx, bias)


def _out_kernel(o_ref, z_ref, x_ref, mod_ref, gpost_ref, w_ref, out_ref):
    tm, d = x_ref.shape
    n_groups = d // LANES
    o = jnp.concatenate([o_ref[0, p] for p in range(n_groups)], axis=-1).astype(F32)
    yv = (_silu(z_ref[...].astype(F32)) * o).astype(BF16)
    y = _dot(yv, w_ref[...])
    gt = mod_ref[0][:, 2 * d:]
    out_ref[...] = x_ref[...] + gt * (_rms(y) * gpost_ref[...])


def _out_call(o, z, x2, mod, g_post, w_out, *, seq_len, tm):
    n_tok, d = x2.shape
    tps = seq_len // tm
    n_seq = n_tok // seq_len
    n_groups = d // LANES
    const = lambda b, t: (0, 0)
    return pl.pallas_call(
        _out_kernel,
        grid=(n_seq, tps),
        in_specs=[
            pl.BlockSpec((1, n_groups, tm, LANES), lambda b, t: (b, 0, t, 0)),
            pl.BlockSpec((tm, d), lambda b, t: (b * tps + t, 0)),
            pl.BlockSpec((tm, d), lambda b, t: (b * tps + t, 0)),
            pl.BlockSpec((1, 1, 3 * d), lambda b, t: (b, 0, 0)),
            pl.BlockSpec((1, d), const),
            pl.BlockSpec((d, d), const),
        ],
        out_specs=pl.BlockSpec((tm, d), lambda b, t: (b * tps + t, 0)),
        out_shape=jax.ShapeDtypeStruct((n_tok, d), F32),
        compiler_params=_params(2),
        name="na_out",
    )(o, z, x2, mod, g_post, w_out)


def kernel(x, c, ctx, c_ctx, g_pre, g_post, w_mod, b_mod, w_in_conv, conv_w, w_out_conv,
           w_in_na, rpb, w_out_na):
    bsz, seq_len, d = x.shape
    _, ctx_len, _ = ctx.shape
    rows = seq_len // GRID_W
    assert g_pre.shape[0] == 2 and d == N_HEADS * HEAD_DIM
    assert rows % (Q_ROWS * GROUPS_PER_STEP) == 0 and rows >= N_KEY_TILES * Q_ROWS
    assert ctx_len == Q_TOK

    mod_rows = -(-(bsz + 1) // HALO) * HALO
    cvec = jnp.zeros((mod_rows, d), F32).at[:bsz].set(c).at[bsz].set(c_ctx)
    mods = _mod_call(cvec, w_mod, b_mod)
    mod_lat = [mods[i, :bsz].reshape(bsz, 1, 3 * d) for i in range(2)]
    mod_ctx = [mods[i, bsz:bsz + 1].reshape(1, 1, 3 * d) for i in range(2)]

    gp = [g_pre[i].reshape(1, d) for i in range(2)]
    gq = [g_post[i].reshape(1, d) for i in range(2)]

    w_in0 = w_in_conv[0].astype(BF16)
    w_out0 = w_out_conv[0].astype(BF16)
    x2 = x.reshape(bsz * seq_len, d)
    c2 = ctx.reshape(bsz * ctx_len, d)
    x1 = _conv_call(x2, mod_lat[0], gp[0], gq[0], w_in0, conv_w[0], w_out0,
                    seq_len=seq_len, tm=512, mod_per_seq=True)
    c1 = _conv_call(c2, mod_ctx[0], gp[0], gq[0], w_in0, conv_w[0], w_out0,
                    seq_len=ctx_len, tm=ctx_len, mod_per_seq=False)

    wi = w_in_na[0]
    wq = wi[:, :d].astype(BF16)
    wkt = wi[:, d:2 * d].T.astype(BF16)
    wv = wi[:, 2 * d:3 * d].astype(BF16)
    wz = wi[:, 3 * d:].astype(BF16)
    q, kt, v, z = _proj_call(x1, mod_lat[1], gp[1], wq, wkt, wv, wz,
                             n_seq=bsz, seq_len=seq_len, tm=512, mod_per_seq=True)
    _, kct, vctx, _ = _proj_call(c1, mod_ctx[1], gp[1], wq, wkt, wv, wz,
                                 n_seq=bsz, seq_len=ctx_len, tm=ctx_len, mod_per_seq=False)
    o = _attn_call(q, kt, v, kct, vctx, _bias_tables(rpb[0]))
    out = _out_call(o, z, x1, mod_lat[1], gq[1], w_out_na[0].astype(BF16), seq_len=seq_len, tm=512)
    return out.reshape(bsz, seq_len, d)
```

```python
import functools

import jax
import jax.numpy as jnp
from jax import lax
from jax.experimental import pallas as pl
from jax.experimental.pallas import tpu as pltpu

F32 = jnp.float32
BF16 = jnp.bfloat16

GRID_W = 64
N_HEADS = 16
HEAD_DIM = 64
WIN_R = 8
WIN_C = 16
EPS = 1e-6
NEG = -1e30

LANES = 128
HEADS_PER_GROUP = LANES // HEAD_DIM
HALO = 8
Q_ROWS = 4
Q_TOK = Q_ROWS * GRID_W
N_KEY_TILES = 3
GROUPS_PER_STEP = 4
STRIP = 16
LOG2E = 1.4426950408889634
VMEM_LIMIT = 56 * 1024 * 1024


def _params(n_axes):
    return pltpu.CompilerParams(dimension_semantics=("arbitrary",) * n_axes,
                                vmem_limit_bytes=VMEM_LIMIT)


def _silu(v):
    return v * jax.nn.sigmoid(v)


def _rms(v):
    return v * lax.rsqrt(jnp.mean(v * v, axis=-1, keepdims=True) + EPS)


def _dot(a, b):
    return jnp.dot(a, b, preferred_element_type=F32)


def _mod_kernel(c_ref, w_ref, b_ref, o_ref):
    s = _silu(c_ref[...])
    o_ref[0] = jnp.dot(s, w_ref[0], preferred_element_type=F32,
                       precision=lax.Precision.HIGHEST) + b_ref[0]


def _mod_call(cvec, w_mod, b_mod):
    depth, d, d3 = w_mod.shape
    rows = cvec.shape[0]
    tn = 1024
    return pl.pallas_call(
        _mod_kernel,
        grid=(depth, d3 // tn),
        in_specs=[
            pl.BlockSpec((rows, d), lambda i, n: (0, 0)),
            pl.BlockSpec((1, d, tn), lambda i, n: (i, 0, n)),
            pl.BlockSpec((1, 1, tn), lambda i, n: (i, 0, n)),
        ],
        out_specs=pl.BlockSpec((1, rows, tn), lambda i, n: (i, 0, n)),
        out_shape=jax.ShapeDtypeStruct((depth, rows, d3), F32),
        compiler_params=_params(2),
        name="mod",
    )(cvec, w_mod, b_mod.reshape(depth, 1, d3))


def _norm_mod(xv, g, mod, d):
    sh = mod[:, :d]
    sc = mod[:, d:2 * d]
    return _rms(xv) * g * (1.0 + sc) + sh


def _conv_kernel(x_ref, xp_ref, xn_ref, mod_ref, gpre_ref, gpost_ref, win_ref, cw_ref, wout_ref,
                 o_ref, yv_ref, *, tiles_per_seq, cb):
    tm, d = x_ref.shape
    ti = pl.program_id(0) % tiles_per_seq
    mod = mod_ref[0]
    g = gpre_ref[...]
    xm = x_ref[...]
    hm = _norm_mod(xm, g, mod, d)
    keep_p = (ti > 0).astype(F32)
    keep_n = (ti < tiles_per_seq - 1).astype(F32)
    hp = _norm_mod(xp_ref[...], g, mod, d)
    hn = _norm_mod(xn_ref[...], g, mod, d)
    hcat = jnp.concatenate([hp, hm, hn], axis=0).astype(BF16)
    hmb = hm.astype(BF16)
    rid = lax.broadcasted_iota(jnp.int32, (tm + 2 * HALO, 1), 0)
    rowkeep = jnp.where(rid < HALO, keep_p, jnp.where(rid >= tm + HALO, keep_n, 1.0))
    for nb in range(d // cb):
        c0 = nb * cb
        cg = _dot(hcat, win_ref[:, d + c0:d + c0 + cb])
        u = _dot(hcat, win_ref[:, 2 * d + c0:2 * d + c0 + cb])
        cu = cg * u * rowkeep
        w = cw_ref[:, c0:c0 + cb]
        conv = (w[0:1] * cu[HALO - 1:HALO - 1 + tm]
                + w[1:2] * cu[HALO:HALO + tm]
                + w[2:3] * cu[HALO + 1:HALO + 1 + tm])
        bg = _dot(hmb, win_ref[:, c0:c0 + cb])
        z = _dot(hmb, win_ref[:, 3 * d + c0:3 * d + c0 + cb])
        yv_ref[:, c0:c0 + cb] = (bg * conv * _silu(z)).astype(BF16)
    y = _dot(yv_ref[...], wout_ref[...])
    gt = mod[:, 2 * d:]
    o_ref[...] = xm + gt * (_rms(y) * gpost_ref[...])


def _conv_call(x2, mod, g_pre, g_post, w_in, conv_w, w_out, *, seq_len, tm, mod_per_seq):
    n_tok, d = x2.shape
    tiles_per_seq = seq_len // tm
    n_tiles = n_tok // tm
    hb = tm // HALO
    last_hb = n_tok // HALO - 1
    if mod_per_seq:
        mod_map = lambda i: (i // tiles_per_seq, 0, 0)
    else:
        mod_map = lambda i: (0, 0, 0)
    const = lambda i: (0, 0)
    return pl.pallas_call(
        functools.partial(_conv_kernel, tiles_per_seq=tiles_per_seq, cb=256),
        grid=(n_tiles,),
        in_specs=[
            pl.BlockSpec((tm, d), lambda i: (i, 0)),
            pl.BlockSpec((HALO, d), lambda i: (jnp.maximum(i * hb - 1, 0), 0)),
            pl.BlockSpec((HALO, d), lambda i: (jnp.minimum((i + 1) * hb, last_hb), 0)),
            pl.BlockSpec((1, 1, 3 * d), mod_map),
            pl.BlockSpec((1, d), const),
            pl.BlockSpec((1, d), const),
            pl.BlockSpec((d, 4 * d), const),
            pl.BlockSpec((3, d), const),
            pl.BlockSpec((d, d), const),
        ],
        out_specs=pl.BlockSpec((tm, d), lambda i: (i, 0)),
        out_shape=jax.ShapeDtypeStruct((n_tok, d), F32),
        scratch_shapes=[pltpu.VMEM((tm, d), BF16)],
        compiler_params=_params(1),
        name="conv_layer",
    )(x2, x2, x2, mod, g_pre, g_post, w_in, conv_w, w_out)


def _proj_kernel(x_ref, mod_ref, gpre_ref, wq_ref, wkt_ref, wv_ref, wz_ref,
                 q_ref, kt_ref, v_ref, z_ref):
    tm, d = x_ref.shape
    n_groups = d // LANES
    hb = _norm_mod(x_ref[...], gpre_ref[...], mod_ref[0], d).astype(BF16)
    q = _dot(hb, wq_ref[...]) * (HEAD_DIM ** -0.5 * LOG2E)
    for p in range(n_groups):
        q_ref[0, p] = q[:, p * LANES:(p + 1) * LANES].astype(BF16)
    v = _dot(hb, wv_ref[...])
    for p in range(n_groups):
        v_ref[0, p] = v[:, p * LANES:(p + 1) * LANES].astype(BF16)
    z_ref[...] = _dot(hb, wz_ref[...]).astype(BF16)
    kt = lax.dot_general(wkt_ref[...], hb, (((1,), (1,)), ((), ())), preferred_element_type=F32)
    kt_ref[0] = kt.astype(BF16)


def _proj_call(x2, mod, g_pre, wq, wkt, wv, wz, *, n_seq, seq_len, tm, mod_per_seq):
    n_tok, d = x2.shape
    tps = seq_len // tm
    n_groups = d // LANES
    if mod_per_seq:
        mod_map = lambda b, t: (b, 0, 0)
    else:
        mod_map = lambda b, t: (0, 0, 0)
    const = lambda b, t: (0, 0)
    grouped = jax.ShapeDtypeStruct((n_seq, n_groups, seq_len, LANES), BF16)
    return pl.pallas_call(
        _proj_kernel,
        grid=(n_seq, tps),
        in_specs=[
            pl.BlockSpec((tm, d), lambda b, t: (b * tps + t, 0)),
            pl.BlockSpec((1, 1, 3 * d), mod_map),
            pl.BlockSpec((1, d), const),
            pl.BlockSpec((d, d), const),
            pl.BlockSpec((d, d), const),
            pl.BlockSpec((d, d), const),
            pl.BlockSpec((d, d), const),
        ],
        out_specs=[
            pl.BlockSpec((1, n_groups, tm, LANES), lambda b, t: (b, 0, t, 0)),
            pl.BlockSpec((1, d, tm), lambda b, t: (b, 0, t)),
            pl.BlockSpec((1, n_groups, tm, LANES), lambda b, t: (b, 0, t, 0)),
            pl.BlockSpec((tm, d), lambda b, t: (b * tps + t, 0)),
        ],
        out_shape=[
            grouped,
            jax.ShapeDtypeStruct((n_seq, d, seq_len), BF16),
            grouped,
            jax.ShapeDtypeStruct((n_tok, d), BF16),
        ],
        compiler_params=_params(2),
        name="na_proj",
    )(x2, mod, g_pre, wq, wkt, wv, wz)


def _bias_tables(rpb):
    h = rpb.shape[0]
    c = jnp.arange(GRID_W)
    cs = jnp.clip(c - WIN_C // 2, 0, GRID_W - WIN_C)
    kc = jnp.arange(GRID_W)
    col_ok = (kc[None, :] >= cs[:, None]) & (kc[None, :] < cs[:, None] + WIN_C)
    dc = jnp.clip(kc[None, :] - c[:, None] + (WIN_C - 1), 0, 2 * WIN_C - 2)
    cb = jnp.where(col_ok[None, None], rpb[:, :, dc], NEG)
    qr = jnp.arange(Q_ROWS)[:, None]
    kv = jnp.arange(N_KEY_TILES * Q_ROWS)[None, :]
    dr = jnp.clip(kv - qr + (WIN_R - 1 - Q_ROWS), 0, 2 * WIN_R - 2)
    ok = jnp.stack([
        jnp.broadcast_to(kv >= Q_ROWS, dr.shape),
        (kv >= qr) & (kv < qr + WIN_R),
        jnp.broadcast_to(kv < WIN_R, dr.shape),
    ])
    t = cb[:, dr] * LOG2E
    t = jnp.where(ok[:, None, :, :, None, None], t[None], NEG)
    t = jnp.transpose(t, (0, 1, 2, 4, 3, 5))
    return t.reshape(3, h // HEADS_PER_GROUP, HEADS_PER_GROUP, Q_TOK, N_KEY_TILES * Q_TOK)


def _attn_kernel(q_ref, ktp_ref, ktc_ref, ktn_ref, vp_ref, vc_ref, vn_ref, kct_ref, vctx_ref,
                 bias_ref, o_ref, kt_s, v_s, s_s, p_s, *, steps_per_seq):
    s = pl.program_id(2)
    step_tok = GROUPS_PER_STEP * Q_TOK
    lat = N_KEY_TILES * Q_TOK
    lane = lax.broadcasted_iota(jnp.int32, (Q_TOK, LANES), 1)

    kt_s[:, 0:Q_TOK] = ktp_ref[0]
    kt_s[:, Q_TOK:Q_TOK + step_tok] = ktc_ref[0]
    kt_s[:, Q_TOK + step_tok:] = ktn_ref[0]
    for a in range(HEADS_PER_GROUP):
        def own(vv):
            ln = lax.broadcasted_iota(jnp.int32, vv.shape, 1)
            keep = (ln >= a * HEAD_DIM) & (ln < (a + 1) * HEAD_DIM)
            return jnp.where(keep, vv, jnp.ones_like(vv))
        v_s[a, 0:Q_TOK, :] = own(vp_ref[0, 0])
        v_s[a, Q_TOK:Q_TOK + step_tok, :] = own(vc_ref[0, 0])
        v_s[a, Q_TOK + step_tok:2 * Q_TOK + step_tok, :] = own(vn_ref[0, 0])
        v_s[a, 2 * Q_TOK + step_tok:, :] = own(vctx_ref[0, 0])

    def scores(u):
        gi, a = divmod(u, HEADS_PER_GROUP)
        qg = q_ref[0, 0, gi * Q_TOK:(gi + 1) * Q_TOK, :]
        in_head = (lane >= a * HEAD_DIM) & (lane < (a + 1) * HEAD_DIM)
        qa = jnp.where(in_head, qg, jnp.zeros_like(qg))
        s_s[u, :, 0:lat] = _dot(qa, kt_s[:, gi * Q_TOK:gi * Q_TOK + lat])
        s_s[u, :, lat:] = _dot(qa, kct_ref[0])

    units = GROUPS_PER_STEP * HEADS_PER_GROUP
    accs = []
    scores(0)
    for u in range(units):
        gi, a = divmod(u, HEADS_PER_GROUP)
        if u + 1 < units:
            scores(u + 1)
        if gi == 0:
            pat = jnp.where(s == 0, 0, 1)
        elif gi == GROUPS_PER_STEP - 1:
            pat = jnp.where(s == steps_per_seq - 1, 2, 1)
        else:
            pat = 1
        for r0 in range(0, Q_TOK, STRIP):
            sl = s_s[u, r0:r0 + STRIP, 0:lat] + bias_ref[pat, 0, a, r0:r0 + STRIP, :]
            sx = s_s[u, r0:r0 + STRIP, lat:]
            m = jnp.maximum(jnp.max(sl, axis=-1, keepdims=True),
                            jnp.max(sx, axis=-1, keepdims=True))
            p_s[u, r0:r0 + STRIP, 0:lat] = jnp.exp2(sl - m).astype(BF16)
            p_s[u, r0:r0 + STRIP, lat:] = jnp.exp2(sx - m).astype(BF16)
        acc = _dot(p_s[u, :, 0:lat], v_s[a, gi * Q_TOK:gi * Q_TOK + lat, :])
        acc = acc + _dot(p_s[u, :, lat:], v_s[a, 2 * Q_TOK + step_tok:, :])
        accs.append(acc / pltpu.roll(acc, HEAD_DIM, axis=1))
        if a == HEADS_PER_GROUP - 1:
            o = jnp.where(lane < HEAD_DIM, accs[0], accs[1])
            o_ref[0, 0, gi * Q_TOK:(gi + 1) * Q_TOK, :] = o.astype(BF16)
            accs = []


def _attn_call(q, kt, v, kct, vctx, bias):
    n_seq, n_groups, seq_len, _ = q.shape
    ctx_len = kct.shape[-1]
    step_tok = GROUPS_PER_STEP * Q_TOK
    steps = seq_len // step_tok
    n_tiles = seq_len // Q_TOK
    units = GROUPS_PER_STEP * HEADS_PER_GROUP
    lat = N_KEY_TILES * Q_TOK
    prev_t = lambda s: jnp.maximum(s * GROUPS_PER_STEP - 1, 0)
    next_t = lambda s: jnp.minimum((s + 1) * GROUPS_PER_STEP, n_tiles - 1)
    return pl.pallas_call(
        functools.partial(_attn_kernel, steps_per_seq=steps),
        grid=(n_groups, n_seq, steps),
        in_specs=[
            pl.BlockSpec((1, 1, step_tok, LANES), lambda p, b, s: (b, p, s, 0)),
            pl.BlockSpec((1, LANES, Q_TOK), lambda p, b, s: (b, p, prev_t(s))),
            pl.BlockSpec((1, LANES, step_tok), lambda p, b, s: (b, p, s)),
            pl.BlockSpec((1, LANES, Q_TOK), lambda p, b, s: (b, p, next_t(s))),
            pl.BlockSpec((1, 1, Q_TOK, LANES), lambda p, b, s: (b, p, prev_t(s), 0)),
            pl.BlockSpec((1, 1, step_tok, LANES), lambda p, b, s: (b, p, s, 0)),
            pl.BlockSpec((1, 1, Q_TOK, LANES), lambda p, b, s: (b, p, next_t(s), 0)),
            pl.BlockSpec((1, LANES, ctx_len), lambda p, b, s: (b, p, 0)),
            pl.BlockSpec((1, 1, ctx_len, LANES), lambda p, b, s: (b, p, 0, 0)),
            pl.BlockSpec((3, 1, HEADS_PER_GROUP, Q_TOK, lat), lambda p, b, s: (0, p, 0, 0, 0)),
        ],
        out_specs=pl.BlockSpec((1, 1, step_tok, LANES), lambda p, b, s: (b, p, s, 0)),
        out_shape=jax.ShapeDtypeStruct(q.shape, BF16),
        scratch_shapes=[
            pltpu.VMEM((LANES, step_tok + 2 * Q_TOK), BF16),
            pltpu.VMEM((HEADS_PER_GROUP, step_tok + 2 * Q_TOK + ctx_len, LANES), BF16),
            pltpu.VMEM((units, Q_TOK, lat + ctx_len), F32),
            pltpu.VMEM((units, Q_TOK, lat + ctx_len), BF16),
        ],
        compiler_params=_params(3),
        name="na_attn",
    )(q, kt, kt, kt, v, v, v, kct, vctx, bias)


def _out_kernel(o_ref, z_ref, x_ref, mod_ref, gpost_ref, w_ref, out_ref):
    tm, d = x_ref.shape
    n_groups = d // LANES
    o = jnp.concatenate([o_ref[0, p] for p in range(n_groups)], axis=-1).astype(F32)
    yv = (_silu(z_ref[...].astype(F32)) * o).astype(BF16)
    y = _dot(yv, w_ref[...])
    gt = mod_ref[0][:, 2 * d:]
    out_ref[...] = x_ref[...] + gt * (_rms(y) * gpost_ref[...])


def _out_call(o, z, x2, mod, g_post, w_out, *, seq_len, tm):
    n_tok, d = x2.shape
    tps = seq_len // tm
    n_seq = n_tok // seq_len
    n_groups = d // LANES
    const = lambda b, t: (0, 0)
    return pl.pallas_call(
        _out_kernel,
        grid=(n_seq, tps),
        in_specs=[
            pl.BlockSpec((1, n_groups, tm, LANES), lambda b, t: (b, 0, t, 0)),
            pl.BlockSpec((tm, d), lambda b, t: (b * tps + t, 0)),
            pl.BlockSpec((tm, d), lambda b, t: (b * tps + t, 0)),
            pl.BlockSpec((1, 1, 3 * d), lambda b, t: (b, 0, 0)),
            pl.BlockSpec((1, d), const),
            pl.BlockSpec((d, d), const),
        ],
        out_specs=pl.BlockSpec((tm, d), lambda b, t: (b * tps + t, 0)),
        out_shape=jax.ShapeDtypeStruct((n_tok, d), F32),
        compiler_params=_params(2),
        name="na_out",
    )(o, z, x2, mod, g_post, w_out)


def kernel(x, c, ctx, c_ctx, g_pre, g_post, w_mod, b_mod, w_in_conv, conv_w, w_out_conv,
           w_in_na, rpb, w_out_na):
    bsz, seq_len, d = x.shape
    _, ctx_len, _ = ctx.shape
    rows = seq_len // GRID_W
    assert g_pre.shape[0] == 2 and d == N_HEADS * HEAD_DIM
    assert rows % (Q_ROWS * GROUPS_PER_STEP) == 0 and rows >= N_KEY_TILES * Q_ROWS
    assert ctx_len == Q_TOK

    mod_rows = -(-(bsz + 1) // HALO) * HALO
    cvec = jnp.zeros((mod_rows, d), F32).at[:bsz].set(c).at[bsz].set(c_ctx)
    mods = _mod_call(cvec, w_mod, b_mod)
    mod_lat = [mods[i, :bsz].reshape(bsz, 1, 3 * d) for i in range(2)]
    mod_ctx = [mods[i, bsz:bsz + 1].reshape(1, 1, 3 * d) for i in range(2)]

    gp = [g_pre[i].reshape(1, d) for i in range(2)]
    gq = [g_post[i].reshape(1, d) for i in range(2)]

    w_in0 = w_in_conv[0].astype(BF16)
    w_out0 = w_out_conv[0].astype(BF16)
    x2 = x.reshape(bsz * seq_len, d)
    c2 = ctx.reshape(bsz * ctx_len, d)
    x1 = _conv_call(x2, mod_lat[0], gp[0], gq[0], w_in0, conv_w[0], w_out0,
                    seq_len=seq_len, tm=512, mod_per_seq=True)
    c1 = _conv_call(c2, mod_ctx[0], gp[0], gq[0], w_in0, conv_w[0], w_out0,
                    seq_len=ctx_len, tm=ctx_len, mod_per_seq=False)

    wi = w_in_na[0]
    wq = wi[:, :d].astype(BF16)
    wkt = wi[:, d:2 * d].T.astype(BF16)
    wv = wi[:, 2 * d:3 * d].astype(BF16)
    wz = wi[:, 3 * d:].astype(BF16)
    q, kt, v, z = _proj_call(x1, mod_lat[1], gp[1], wq, wkt, wv, wz,
                             n_seq=bsz, seq_len=seq_len, tm=512, mod_per_seq=True)
    _, kct, vctx, _ = _proj_call(c1, mod_ctx[1], gp[1], wq, wkt, wv, wz,
                                 n_seq=bsz, seq_len=ctx_len, tm=ctx_len, mod_per_seq=False)
    o = _attn_call(q, kt, v, kct, vctx, _bias_tables(rpb[0]))
    out = _out_call(o, z, x1, mod_lat[1], gq[1], w_out_na[0].astype(BF16), seq_len=seq_len, tm=512)
    return out.reshape(bsz, seq_len, d)
```

```python
import functools

import jax
import jax.numpy as jnp
import numpy as np
from jax import lax
from jax.experimental import pallas as pl
from jax.experimental.pallas import tpu as pltpu

F32 = jnp.float32
BF16 = jnp.bfloat16

GRID_W = 64
N_HEADS = 16
HEAD_DIM = 64
WIN_R = 8
WIN_C = 16
EPS = 1e-6
NEG = -1e30

LANES = 128
HEADS_PER_GROUP = LANES // HEAD_DIM
HALO = 8
Q_ROWS = 4
Q_TOK = Q_ROWS * GRID_W
BAND = WIN_R * GRID_W
ROWS_PER_STEP = 32
STRIP = 16
LOG2E = 1.4426950408889634
VMEM_LIMIT = 56 * 1024 * 1024


def _params(n_axes):
    return pltpu.CompilerParams(dimension_semantics=("arbitrary",) * n_axes,
                                vmem_limit_bytes=VMEM_LIMIT)


def _silu(v):
    return v * jax.nn.sigmoid(v)


def _rms(v):
    return v * lax.rsqrt(jnp.mean(v * v, axis=-1, keepdims=True) + EPS)


def _dot(a, b):
    return jnp.dot(a, b, preferred_element_type=F32)


def _mod_kernel(c_ref, w_ref, b_ref, o_ref):
    s = _silu(c_ref[...])
    o_ref[0] = jnp.dot(s, w_ref[0], preferred_element_type=F32,
                       precision=lax.Precision.HIGHEST) + b_ref[0]


def _mod_call(cvec, w_mod, b_mod):
    depth, d, d3 = w_mod.shape
    rows = cvec.shape[0]
    tn = 1024
    return pl.pallas_call(
        _mod_kernel,
        grid=(depth, d3 // tn),
        in_specs=[
            pl.BlockSpec((rows, d), lambda i, n: (0, 0)),
            pl.BlockSpec((1, d, tn), lambda i, n: (i, 0, n)),
            pl.BlockSpec((1, 1, tn), lambda i, n: (i, 0, n)),
        ],
        out_specs=pl.BlockSpec((1, rows, tn), lambda i, n: (i, 0, n)),
        out_shape=jax.ShapeDtypeStruct((depth, rows, d3), F32),
        compiler_params=_params(2),
        name="mod",
    )(cvec, w_mod, b_mod.reshape(depth, 1, d3))


def _norm_mod(xv, g, mod, d):
    sh = mod[:, :d]
    sc = mod[:, d:2 * d]
    return _rms(xv) * g * (1.0 + sc) + sh


def _conv_kernel(x_ref, xp_ref, xn_ref, mod_ref, gpre_ref, gpost_ref, win_ref, cw_ref, wout_ref,
                 o_ref, yv_ref, *, tiles_per_seq, cb):
    tm, d = x_ref.shape
    ti = pl.program_id(0) % tiles_per_seq
    mod = mod_ref[0]
    g = gpre_ref[...]
    xm = x_ref[...]
    hm = _norm_mod(xm, g, mod, d)
    keep_p = (ti > 0).astype(F32)
    keep_n = (ti < tiles_per_seq - 1).astype(F32)
    hp = _norm_mod(xp_ref[...], g, mod, d)
    hn = _norm_mod(xn_ref[...], g, mod, d)
    hcat = jnp.concatenate([hp, hm, hn], axis=0).astype(BF16)
    hmb = hm.astype(BF16)
    rid = lax.broadcasted_iota(jnp.int32, (tm + 2 * HALO, 1), 0)
    rowkeep = jnp.where(rid < HALO, keep_p, jnp.where(rid >= tm + HALO, keep_n, 1.0))
    for nb in range(d // cb):
        c0 = nb * cb
        cg = _dot(hcat, win_ref[:, d + c0:d + c0 + cb])
        u = _dot(hcat, win_ref[:, 2 * d + c0:2 * d + c0 + cb])
        cu = cg * u * rowkeep
        w = cw_ref[:, c0:c0 + cb]
        conv = (w[0:1] * cu[HALO - 1:HALO - 1 + tm]
                + w[1:2] * cu[HALO:HALO + tm]
                + w[2:3] * cu[HALO + 1:HALO + 1 + tm])
        bg = _dot(hmb, win_ref[:, c0:c0 + cb])
        z = _dot(hmb, win_ref[:, 3 * d + c0:3 * d + c0 + cb])
        yv_ref[:, c0:c0 + cb] = (bg * conv * _silu(z)).astype(BF16)
    y = _dot(yv_ref[...], wout_ref[...])
    gt = mod[:, 2 * d:]
    o_ref[...] = xm + gt * (_rms(y) * gpost_ref[...])


def _conv_call(x2, mod, g_pre, g_post, w_in, conv_w, w_out, *, seq_len, tm, mod_per_seq):
    n_tok, d = x2.shape
    tiles_per_seq = seq_len // tm
    n_tiles = n_tok // tm
    hb = tm // HALO
    last_hb = n_tok // HALO - 1
    if mod_per_seq:
        mod_map = lambda i: (i // tiles_per_seq, 0, 0)
    else:
        mod_map = lambda i: (0, 0, 0)
    const = lambda i: (0, 0)
    return pl.pallas_call(
        functools.partial(_conv_kernel, tiles_per_seq=tiles_per_seq, cb=256),
        grid=(n_tiles,),
        in_specs=[
            pl.BlockSpec((tm, d), lambda i: (i, 0)),
            pl.BlockSpec((HALO, d), lambda i: (jnp.maximum(i * hb - 1, 0), 0)),
            pl.BlockSpec((HALO, d), lambda i: (jnp.minimum((i + 1) * hb, last_hb), 0)),
            pl.BlockSpec((1, 1, 3 * d), mod_map),
            pl.BlockSpec((1, d), const),
            pl.BlockSpec((1, d), const),
            pl.BlockSpec((d, 4 * d), const),
            pl.BlockSpec((3, d), const),
            pl.BlockSpec((d, d), const),
        ],
        out_specs=pl.BlockSpec((tm, d), lambda i: (i, 0)),
        out_shape=jax.ShapeDtypeStruct((n_tok, d), F32),
        scratch_shapes=[pltpu.VMEM((tm, d), BF16)],
        compiler_params=_params(1),
        name="conv_layer",
    )(x2, x2, x2, mod, g_pre, g_post, w_in, conv_w, w_out)


def _proj_kernel(x_ref, mod_ref, gpre_ref, wq_ref, wkt_ref, wv_ref, wz_ref,
                 q_ref, kt_ref, v_ref, z_ref):
    tm, d = x_ref.shape
    n_groups = d // LANES
    hb = _norm_mod(x_ref[...], gpre_ref[...], mod_ref[0], d).astype(BF16)
    q = _dot(hb, wq_ref[...]) * (HEAD_DIM ** -0.5 * LOG2E)
    for p in range(n_groups):
        q_ref[0, p] = q[:, p * LANES:(p + 1) * LANES].astype(BF16)
    v = _dot(hb, wv_ref[...])
    for p in range(n_groups):
        v_ref[0, p] = v[:, p * LANES:(p + 1) * LANES].astype(BF16)
    z_ref[...] = _dot(hb, wz_ref[...]).astype(BF16)
    kt = lax.dot_general(wkt_ref[...], hb, (((1,), (1,)), ((), ())), preferred_element_type=F32)
    kt_ref[0] = kt.astype(BF16)


def _proj_call(x2, mod, g_pre, wq, wkt, wv, wz, *, n_seq, seq_len, tm, mod_per_seq):
    n_tok, d = x2.shape
    tps = seq_len // tm
    n_groups = d // LANES
    if mod_per_seq:
        mod_map = lambda b, t: (b, 0, 0)
    else:
        mod_map = lambda b, t: (0, 0, 0)
    const = lambda b, t: (0, 0)
    grouped = jax.ShapeDtypeStruct((n_seq, n_groups, seq_len, LANES), BF16)
    return pl.pallas_call(
        _proj_kernel,
        grid=(n_seq, tps),
        in_specs=[
            pl.BlockSpec((tm, d), lambda b, t: (b * tps + t, 0)),
            pl.BlockSpec((1, 1, 3 * d), mod_map),
            pl.BlockSpec((1, d), const),
            pl.BlockSpec((d, d), const),
            pl.BlockSpec((d, d), const),
            pl.BlockSpec((d, d), const),
            pl.BlockSpec((d, d), const),
        ],
        out_specs=[
            pl.BlockSpec((1, n_groups, tm, LANES), lambda b, t: (b, 0, t, 0)),
            pl.BlockSpec((1, d, tm), lambda b, t: (b, 0, t)),
            pl.BlockSpec((1, n_groups, tm, LANES), lambda b, t: (b, 0, t, 0)),
            pl.BlockSpec((tm, d), lambda b, t: (b * tps + t, 0)),
        ],
        out_shape=[
            grouped,
            jax.ShapeDtypeStruct((n_seq, d, seq_len), BF16),
            grouped,
            jax.ShapeDtypeStruct((n_tok, d), BF16),
        ],
        compiler_params=_params(2),
        name="na_proj",
    )(x2, mod, g_pre, wq, wkt, wv, wz)


def _bias_tables(rpb):
    h = rpb.shape[0]
    pad = GRID_W - WIN_C
    rp = jnp.pad(rpb, ((0, 0), (0, 0), (pad, pad)))
    cb = jnp.stack([rp[:, :, GRID_W - 1 - c:2 * GRID_W - 1 - c] for c in range(GRID_W)], axis=2)
    c = np.arange(GRID_W)
    cs = np.clip(c - WIN_C // 2, 0, GRID_W - WIN_C)
    col_ok = (c[None, :] >= cs[:, None]) & (c[None, :] < cs[:, None] + WIN_C)
    cb = jnp.where(jnp.asarray(col_ok)[None, None], cb * LOG2E, NEG)
    tables = []
    for t in range(2 * Q_ROWS + 1):
        drs = []
        for i in range(WIN_R):
            if t < Q_ROWS:
                p = t + i
                key_minus_query = (p + Q_ROWS if p < Q_ROWS else p - Q_ROWS) - t
            elif t == Q_ROWS:
                key_minus_query = i - Q_ROWS
            else:
                jj = t - Q_ROWS - 1
                key_minus_query = i - Q_ROWS if i < WIN_R - jj else i - Q_ROWS - WIN_R
            drs.append(key_minus_query + WIN_R - 1)
        tables.append(jnp.stack([cb[:, dr] for dr in drs], axis=2))
    tb = jnp.stack(tables)
    return tb.reshape(2 * Q_ROWS + 1, h // HEADS_PER_GROUP, HEADS_PER_GROUP, GRID_W, BAND)


def _attn_kernel(q_ref, ktp_ref, ktc_ref, ktn_ref, vp_ref, vc_ref, vn_ref, kct_ref, vctx_ref,
                 bias_ref, o_ref, kt_s, ktsh_s, v_s, qm_s, s_s, p_s, *, steps_per_seq):
    s = pl.program_id(2)
    step_tok = ROWS_PER_STEP * GRID_W
    win_tok = step_tok + 2 * Q_TOK
    n_groups = ROWS_PER_STEP // Q_ROWS
    lane = lax.broadcasted_iota(jnp.int32, (Q_TOK, LANES), 1)

    kt_s[:, 0:Q_TOK] = ktp_ref[0]
    kt_s[:, Q_TOK:Q_TOK + step_tok] = ktc_ref[0]
    kt_s[:, Q_TOK + step_tok:] = ktn_ref[0]
    ku = pltpu.bitcast(kt_s[...], jnp.uint32)
    ktsh_s[...] = pltpu.bitcast(pltpu.roll(ku, win_tok - GRID_W, axis=1), BF16)
    for a in range(HEADS_PER_GROUP):
        def own(vv):
            ln = lax.broadcasted_iota(jnp.int32, vv.shape, 1)
            keep = (ln >= a * HEAD_DIM) & (ln < (a + 1) * HEAD_DIM)
            return jnp.where(keep, vv, jnp.ones_like(vv))
        v_s[a, 0:Q_TOK, :] = own(vp_ref[0, 0])
        v_s[a, Q_TOK:Q_TOK + step_tok, :] = own(vc_ref[0, 0])
        v_s[a, Q_TOK + step_tok:win_tok, :] = own(vn_ref[0, 0])
        v_s[a, win_tok:, :] = own(vctx_ref[0, 0])
    for g in range(n_groups):
        qg = q_ref[0, 0, g * Q_TOK:(g + 1) * Q_TOK, :]
        for a in range(HEADS_PER_GROUP):
            in_head = (lane >= a * HEAD_DIM) & (lane < (a + 1) * HEAD_DIM)
            qm_s[g, a * Q_TOK:(a + 1) * Q_TOK, :] = jnp.where(in_head, qg, jnp.zeros_like(qg))

    def scores(g):
        for jj in range(Q_ROWS):
            j = g * Q_ROWS + jj
            lhs = jnp.concatenate([qm_s[g, a * Q_TOK + jj * GRID_W:a * Q_TOK + (jj + 1) * GRID_W, :]
                                   for a in range(HEADS_PER_GROUP)], axis=0)
            if j % 2 == 0:
                band = kt_s[:, j * GRID_W:j * GRID_W + BAND]
            else:
                band = ktsh_s[:, (j - 1) * GRID_W:(j - 1) * GRID_W + BAND]
            r = _dot(lhs, band)
            for a in range(HEADS_PER_GROUP):
                s_s[g, a * Q_TOK + jj * GRID_W:a * Q_TOK + (jj + 1) * GRID_W, 0:BAND] = (
                    r[a * GRID_W:(a + 1) * GRID_W])
        s_s[g, :, BAND:] = _dot(qm_s[g], kct_ref[0])

    def softmax(g):
        for a in range(HEADS_PER_GROUP):
            for jj in range(Q_ROWS):
                j = g * Q_ROWS + jj
                if j < Q_ROWS:
                    kind = jnp.where(s == 0, j, Q_ROWS)
                elif j >= ROWS_PER_STEP - Q_ROWS:
                    kind = jnp.where(s == steps_per_seq - 1, j - ROWS_PER_STEP + 2 * Q_ROWS + 1, Q_ROWS)
                else:
                    kind = Q_ROWS
                for t0 in range(0, GRID_W, STRIP):
                    r0 = a * Q_TOK + jj * GRID_W + t0
                    sl = s_s[g, r0:r0 + STRIP, 0:BAND] + bias_ref[kind, 0, a, t0:t0 + STRIP, :]
                    sx = s_s[g, r0:r0 + STRIP, BAND:]
                    m = jnp.maximum(jnp.max(sl, axis=-1, keepdims=True),
                                    jnp.max(sx, axis=-1, keepdims=True))
                    p_s[g, r0:r0 + STRIP, 0:BAND] = jnp.exp2(sl - m).astype(BF16)
                    p_s[g, r0:r0 + STRIP, BAND:] = jnp.exp2(sx - m).astype(BF16)

    def values(g):
        outs = []
        for a in range(HEADS_PER_GROUP):
            acc = _dot(p_s[g, a * Q_TOK:(a + 1) * Q_TOK, BAND:], v_s[a, win_tok:, :])
            lat = []
            for jj in range(Q_ROWS):
                j = g * Q_ROWS + jj
                r0 = a * Q_TOK + jj * GRID_W
                lat.append(_dot(p_s[g, r0:r0 + GRID_W, 0:BAND], v_s[a, j * GRID_W:j * GRID_W + BAND, :]))
            acc = acc + jnp.concatenate(lat, axis=0)
            outs.append(acc / pltpu.roll(acc, HEAD_DIM, axis=1))
        o = jnp.where(lane < HEAD_DIM, outs[0], outs[1])
        o_ref[0, 0, g * Q_TOK:(g + 1) * Q_TOK, :] = o.astype(BF16)

    scores(0)
    for g in range(n_groups):
        if g + 1 < n_groups:
            scores(g + 1)
        softmax(g)
        values(g)


def _attn_call(q, kt, v, kct, vctx, bias):
    n_seq, n_pairs, seq_len, _ = q.shape
    ctx_len = kct.shape[-1]
    step_tok = ROWS_PER_STEP * GRID_W
    win_tok = step_tok + 2 * Q_TOK
    steps = seq_len // step_tok
    n_tiles = seq_len // Q_TOK
    n_groups = ROWS_PER_STEP // Q_ROWS
    prev_t = lambda s: jnp.where(s == 0, 1, s * n_groups - 1)
    next_t = lambda s: jnp.where(s == steps - 1, n_tiles - 2, (s + 1) * n_groups)
    return pl.pallas_call(
        functools.partial(_attn_kernel, steps_per_seq=steps),
        grid=(n_pairs, n_seq, steps),
        in_specs=[
            pl.BlockSpec((1, 1, step_tok, LANES), lambda p, b, s: (b, p, s, 0)),
            pl.BlockSpec((1, LANES, Q_TOK), lambda p, b, s: (b, p, prev_t(s))),
            pl.BlockSpec((1, LANES, step_tok), lambda p, b, s: (b, p, s)),
            pl.BlockSpec((1, LANES, Q_TOK), lambda p, b, s: (b, p, next_t(s))),
            pl.BlockSpec((1, 1, Q_TOK, LANES), lambda p, b, s: (b, p, prev_t(s), 0)),
            pl.BlockSpec((1, 1, step_tok, LANES), lambda p, b, s: (b, p, s, 0)),
            pl.BlockSpec((1, 1, Q_TOK, LANES), lambda p, b, s: (b, p, next_t(s), 0)),
            pl.BlockSpec((1, LANES, ctx_len), lambda p, b, s: (b, p, 0)),
            pl.BlockSpec((1, 1, ctx_len, LANES), lambda p, b, s: (b, p, 0, 0)),
            pl.BlockSpec((2 * Q_ROWS + 1, 1, HEADS_PER_GROUP, GRID_W, BAND),
                         lambda p, b, s: (0, p, 0, 0, 0)),
        ],
        out_specs=pl.BlockSpec((1, 1, step_tok, LANES), lambda p, b, s: (b, p, s, 0)),
        out_shape=jax.ShapeDtypeStruct(q.shape, BF16),
        scratch_shapes=[
            pltpu.VMEM((LANES, win_tok), BF16),
            pltpu.VMEM((LANES, win_tok), BF16),
            pltpu.VMEM((HEADS_PER_GROUP, win_tok + ctx_len, LANES), BF16),
            pltpu.VMEM((n_groups, HEADS_PER_GROUP * Q_TOK, LANES), BF16),
            pltpu.VMEM((n_groups, HEADS_PER_GROUP * Q_TOK, BAND + ctx_len), F32),
            pltpu.VMEM((n_groups, HEADS_PER_GROUP * Q_TOK, BAND + ctx_len), BF16),
        ],
        compiler_params=_params(3),
        name="na_attn",
    )(q, kt, kt, kt, v, v, v, kct, vctx, bias)


def _out_kernel(o_ref, z_ref, x_ref, mod_ref, gpost_ref, w_ref, out_ref):
    tm, d = x_ref.shape
    n_groups = d // LANES
    o = jnp.concatenate([o_ref[0, p] for p in range(n_groups)], axis=-1).astype(F32)
    yv = (_silu(z_ref[...].astype(F32)) * o).astype(BF16)
    y = _dot(yv, w_ref[...])
    gt = mod_ref[0][:, 2 * d:]
    out_ref[...] = x_ref[...] + gt * (_rms(y) * gpost_ref[...])


def _out_call(o, z, x2, mod, g_post, w_out, *, seq_len, tm):
    n_tok, d = x2.shape
    tps = seq_len // tm
    n_seq = n_tok // seq_len
    n_groups = d // LANES
    const = lambda b, t: (0, 0)
    return pl.pallas_call(
        _out_kernel,
        grid=(n_seq, tps),
        in_specs=[
            pl.BlockSpec((1, n_groups, tm, LANES), lambda b, t: (b, 0, t, 0)),
            pl.BlockSpec((tm, d), lambda b, t: (b * tps + t, 0)),
            pl.BlockSpec((tm, d), lambda b, t: (b * tps + t, 0)),
            pl.BlockSpec((1, 1, 3 * d), lambda b, t: (b, 0, 0)),
            pl.BlockSpec((1, d), const),
            pl.BlockSpec((d, d), const),
        ],
        out_specs=pl.BlockSpec((tm, d), lambda b, t: (b * tps + t, 0)),
        out_shape=jax.ShapeDtypeStruct((n_tok, d), F32),
        compiler_params=_params(2),
        name="na_out",
    )(o, z, x2, mod, g_post, w_out)


def kernel(x, c, ctx, c_ctx, g_pre, g_post, w_mod, b_mod, w_in_conv, conv_w, w_out_conv,
           w_in_na, rpb, w_out_na):
    bsz, seq_len, d = x.shape
    _, ctx_len, _ = ctx.shape
    rows = seq_len // GRID_W
    assert g_pre.shape[0] == 2 and d == N_HEADS * HEAD_DIM
    assert rows % ROWS_PER_STEP == 0 and ROWS_PER_STEP >= 2 * Q_ROWS and rows >= 3 * Q_ROWS

    mod_rows = -(-(bsz + 1) // HALO) * HALO
    cvec = jnp.zeros((mod_rows, d), F32).at[:bsz].set(c).at[bsz].set(c_ctx)
    mods = _mod_call(cvec, w_mod, b_mod)
    mod_lat = [mods[i, :bsz].reshape(bsz, 1, 3 * d) for i in range(2)]
    mod_ctx = [mods[i, bsz:bsz + 1].reshape(1, 1, 3 * d) for i in range(2)]

    gp = [g_pre[i].reshape(1, d) for i in range(2)]
    gq = [g_post[i].reshape(1, d) for i in range(2)]

    w_in0 = w_in_conv[0].astype(BF16)
    w_out0 = w_out_conv[0].astype(BF16)
    x2 = x.reshape(bsz * seq_len, d)
    c2 = ctx.reshape(bsz * ctx_len, d)
    x1 = _conv_call(x2, mod_lat[0], gp[0], gq[0], w_in0, conv_w[0], w_out0,
                    seq_len=seq_len, tm=512, mod_per_seq=True)
    c1 = _conv_call(c2, mod_ctx[0], gp[0], gq[0], w_in0, conv_w[0], w_out0,
                    seq_len=ctx_len, tm=ctx_len, mod_per_seq=False)

    wi = w_in_na[0]
    wq = wi[:, :d].astype(BF16)
    wkt = wi[:, d:2 * d].T.astype(BF16)
    wv = wi[:, 2 * d:3 * d].astype(BF16)
    wz = wi[:, 3 * d:].astype(BF16)
    q, kt, v, z = _proj_call(x1, mod_lat[1], gp[1], wq, wkt, wv, wz,
                             n_seq=bsz, seq_len=seq_len, tm=512, mod_per_seq=True)
    _, kct, vctx, _ = _proj_call(c1, mod_ctx[1], gp[1], wq, wkt, wv, wz,
                                 n_seq=bsz, seq_len=ctx_len, tm=ctx_len, mod_per_seq=False)
    o = _attn_call(q, kt, v, kct, vctx, _bias_tables(rpb[0]))
    out = _out_call(o, z, x1, mod_lat[1], gq[1], w_out_na[0].astype(BF16), seq_len=seq_len, tm=512)
    return out.reshape(bsz, seq_len, d)
```

```python
import functools

import jax
import jax.numpy as jnp
import numpy as np
from jax import lax
from jax.experimental import pallas as pl
from jax.experimental.pallas import tpu as pltpu

F32 = jnp.float32
BF16 = jnp.bfloat16

GRID_W = 64
N_HEADS = 16
HEAD_DIM = 64
WIN_R = 8
WIN_C = 16
EPS = 1e-6
NEG = -1e30

LANES = 128
HEADS_PER_GROUP = LANES // HEAD_DIM
HALO = 8
Q_ROWS = 4
Q_TOK = Q_ROWS * GRID_W
BAND = WIN_R * GRID_W
ROWS_PER_STEP = 32
STRIP = 16
LOG2E = 1.4426950408889634
VMEM_LIMIT = 56 * 1024 * 1024


def _resident(shape, index_map):
    return pl.BlockSpec(shape, index_map, pipeline_mode=pl.Buffered(1))


def _params(n_axes):
    return pltpu.CompilerParams(dimension_semantics=("arbitrary",) * n_axes,
                                vmem_limit_bytes=VMEM_LIMIT)


def _silu(v):
    return v * jax.nn.sigmoid(v)


def _rms(v):
    return v * lax.rsqrt(jnp.mean(v * v, axis=-1, keepdims=True) + EPS)


def _dot(a, b):
    return jnp.dot(a, b, preferred_element_type=F32)


def _mod_kernel(c_ref, w_ref, b_ref, o_ref):
    s = _silu(c_ref[...])
    o_ref[0] = jnp.dot(s, w_ref[0], preferred_element_type=F32,
                       precision=lax.Precision.HIGHEST) + b_ref[0]


def _mod_call(cvec, w_mod, b_mod):
    depth, d, d3 = w_mod.shape
    rows = cvec.shape[0]
    tn = 1024
    return pl.pallas_call(
        _mod_kernel,
        grid=(depth, d3 // tn),
        in_specs=[
            pl.BlockSpec((rows, d), lambda i, n: (0, 0)),
            pl.BlockSpec((1, d, tn), lambda i, n: (i, 0, n)),
            pl.BlockSpec((1, 1, tn), lambda i, n: (i, 0, n)),
        ],
        out_specs=pl.BlockSpec((1, rows, tn), lambda i, n: (i, 0, n)),
        out_shape=jax.ShapeDtypeStruct((depth, rows, d3), F32),
        compiler_params=_params(2),
        name="mod",
    )(cvec, w_mod, b_mod.reshape(depth, 1, d3))


def _norm_mod(xv, g, mod, d):
    sh = mod[:, :d]
    sc = mod[:, d:2 * d]
    return _rms(xv) * g * (1.0 + sc) + sh


def _conv_kernel(x_ref, xp_ref, xn_ref, mod_ref, gpre_ref, gpost_ref, win_ref, cw_ref, wout_ref,
                 o_ref, yv_ref, *, tiles_per_seq, cb):
    tm, d = x_ref.shape
    ti = pl.program_id(0) % tiles_per_seq
    mod = mod_ref[0]
    g = gpre_ref[...]
    xm = x_ref[...]
    hm = _norm_mod(xm, g, mod, d)
    keep_p = (ti > 0).astype(F32)
    keep_n = (ti < tiles_per_seq - 1).astype(F32)
    hp = _norm_mod(xp_ref[...], g, mod, d)
    hn = _norm_mod(xn_ref[...], g, mod, d)
    hcat = jnp.concatenate([hp, hm, hn], axis=0).astype(BF16)
    hmb = hm.astype(BF16)
    rid = lax.broadcasted_iota(jnp.int32, (tm + 2 * HALO, 1), 0)
    rowkeep = jnp.where(rid < HALO, keep_p, jnp.where(rid >= tm + HALO, keep_n, 1.0))
    for nb in range(d // cb):
        c0 = nb * cb
        cg = _dot(hcat, win_ref[:, d + c0:d + c0 + cb])
        u = _dot(hcat, win_ref[:, 2 * d + c0:2 * d + c0 + cb])
        cu = cg * u * rowkeep
        w = cw_ref[:, c0:c0 + cb]
        conv = (w[0:1] * cu[HALO - 1:HALO - 1 + tm]
                + w[1:2] * cu[HALO:HALO + tm]
                + w[2:3] * cu[HALO + 1:HALO + 1 + tm])
        bg = _dot(hmb, win_ref[:, c0:c0 + cb])
        z = _dot(hmb, win_ref[:, 3 * d + c0:3 * d + c0 + cb])
        yv_ref[:, c0:c0 + cb] = (bg * conv * _silu(z)).astype(BF16)
    gt = mod[:, 2 * d:]
    half = tm // 2
    for r0 in (0, half):
        y = _dot(yv_ref[r0:r0 + half, :], wout_ref[...])
        o_ref[r0:r0 + half, :] = x_ref[r0:r0 + half, :] + gt * (_rms(y) * gpost_ref[...])


def _conv_call(x2, mod, g_pre, g_post, w_in, conv_w, w_out, *, seq_len, tm, mod_per_seq):
    n_tok, d = x2.shape
    tiles_per_seq = seq_len // tm
    n_tiles = n_tok // tm
    hb = tm // HALO
    last_hb = n_tok // HALO - 1
    if mod_per_seq:
        mod_map = lambda i: (i // tiles_per_seq, 0, 0)
    else:
        mod_map = lambda i: (0, 0, 0)
    const = lambda i: (0, 0)
    return pl.pallas_call(
        functools.partial(_conv_kernel, tiles_per_seq=tiles_per_seq, cb=256),
        grid=(n_tiles,),
        in_specs=[
            pl.BlockSpec((tm, d), lambda i: (i, 0)),
            pl.BlockSpec((HALO, d), lambda i: (jnp.maximum(i * hb - 1, 0), 0)),
            pl.BlockSpec((HALO, d), lambda i: (jnp.minimum((i + 1) * hb, last_hb), 0)),
            pl.BlockSpec((1, 1, 3 * d), mod_map),
            pl.BlockSpec((1, d), const),
            pl.BlockSpec((1, d), const),
            _resident((d, 4 * d), const),
            pl.BlockSpec((3, d), const),
            _resident((d, d), const),
        ],
        out_specs=pl.BlockSpec((tm, d), lambda i: (i, 0)),
        out_shape=jax.ShapeDtypeStruct((n_tok, d), F32),
        scratch_shapes=[pltpu.VMEM((tm, d), BF16)],
        compiler_params=_params(1),
        name="conv_layer",
    )(x2, x2, x2, mod, g_pre, g_post, w_in, conv_w, w_out)


def _proj_kernel(x_ref, mod_ref, gpre_ref, wq_ref, wkt_ref, wv_ref, wz_ref,
                 q_ref, kt_ref, v_ref, z_ref):
    tm, d = x_ref.shape
    n_groups = d // LANES
    hb = _norm_mod(x_ref[...], gpre_ref[...], mod_ref[0], d).astype(BF16)
    q = _dot(hb, wq_ref[...]) * (HEAD_DIM ** -0.5 * LOG2E)
    for p in range(n_groups):
        q_ref[0, p] = q[:, p * LANES:(p + 1) * LANES].astype(BF16)
    v = _dot(hb, wv_ref[...])
    for p in range(n_groups):
        v_ref[0, p] = v[:, p * LANES:(p + 1) * LANES].astype(BF16)
    z_ref[...] = _dot(hb, wz_ref[...]).astype(BF16)
    kt = lax.dot_general(wkt_ref[...], hb, (((1,), (1,)), ((), ())), preferred_element_type=F32)
    kt_ref[0] = kt.astype(BF16)


def _proj_call(x2, mod, g_pre, wq, wkt, wv, wz, *, n_seq, seq_len, tm, mod_per_seq):
    n_tok, d = x2.shape
    tps = seq_len // tm
    n_groups = d // LANES
    if mod_per_seq:
        mod_map = lambda b, t: (b, 0, 0)
    else:
        mod_map = lambda b, t: (0, 0, 0)
    const = lambda b, t: (0, 0)
    grouped = jax.ShapeDtypeStruct((n_seq, n_groups, seq_len, LANES), BF16)
    return pl.pallas_call(
        _proj_kernel,
        grid=(n_seq, tps),
        in_specs=[
            pl.BlockSpec((tm, d), lambda b, t: (b * tps + t, 0)),
            pl.BlockSpec((1, 1, 3 * d), mod_map),
            pl.BlockSpec((1, d), const),
            _resident((d, d), const),
            _resident((d, d), const),
            _resident((d, d), const),
            _resident((d, d), const),
        ],
        out_specs=[
            pl.BlockSpec((1, n_groups, tm, LANES), lambda b, t: (b, 0, t, 0)),
            pl.BlockSpec((1, d, tm), lambda b, t: (b, 0, t)),
            pl.BlockSpec((1, n_groups, tm, LANES), lambda b, t: (b, 0, t, 0)),
            pl.BlockSpec((tm, d), lambda b, t: (b * tps + t, 0)),
        ],
        out_shape=[
            grouped,
            jax.ShapeDtypeStruct((n_seq, d, seq_len), BF16),
            grouped,
            jax.ShapeDtypeStruct((n_tok, d), BF16),
        ],
        compiler_params=_params(2),
        name="na_proj",
    )(x2, mod, g_pre, wq, wkt, wv, wz)


def _bias_tables(rpb):
    h = rpb.shape[0]
    n_kinds = 2 * Q_ROWS + 1
    n_dr, n_dc = 2 * WIN_R - 1, 2 * WIN_C - 1
    sel_r = np.zeros((n_kinds, WIN_R, n_dr), np.float32)
    for t in range(n_kinds):
        for i in range(WIN_R):
            if t < Q_ROWS:
                p = t + i
                key_minus_query = (p + Q_ROWS if p < Q_ROWS else p - Q_ROWS) - t
            elif t == Q_ROWS:
                key_minus_query = i - Q_ROWS
            else:
                jj = t - Q_ROWS - 1
                key_minus_query = i - Q_ROWS if i < WIN_R - jj else i - Q_ROWS - WIN_R
            sel_r[t, i, key_minus_query + WIN_R - 1] = 1.0
    c = np.arange(GRID_W)
    cs = np.clip(c - WIN_C // 2, 0, GRID_W - WIN_C)
    col_ok = (c[None, :] >= cs[:, None]) & (c[None, :] < cs[:, None] + WIN_C)
    dc = c[None, :] - c[:, None] + WIN_C - 1
    sel_c = ((dc[None] == np.arange(n_dc)[:, None, None]) & col_ok[None]).astype(np.float32)
    hp = lax.Precision.HIGHEST
    rows = jnp.einsum("tir,hrd->thid", jnp.asarray(sel_r), rpb, precision=hp)
    tb = jnp.einsum("thid,dck->thcik", rows, jnp.asarray(sel_c), precision=hp)
    tb = jnp.where(jnp.asarray(col_ok)[None, None, :, None, :], tb * LOG2E, NEG)
    return tb.reshape(n_kinds, h // HEADS_PER_GROUP, HEADS_PER_GROUP, GRID_W, BAND)


def _attn_kernel(q_ref, ktp_ref, ktc_ref, ktn_ref, vp_ref, vc_ref, vn_ref, kct_ref, vctx_ref,
                 bias_ref, o_ref, kt_s, ktsh_s, v_s, qm_s, s_s, m_s, p_s, *, steps_per_seq):
    s = pl.program_id(2)
    ctx_len = kct_ref.shape[-1]
    step_tok = ROWS_PER_STEP * GRID_W
    win_tok = step_tok + 2 * Q_TOK
    n_groups = ROWS_PER_STEP // Q_ROWS
    lane = lax.broadcasted_iota(jnp.int32, (Q_TOK, LANES), 1)

    kt_s[:, 0:Q_TOK] = ktp_ref[0]
    kt_s[:, Q_TOK:Q_TOK + step_tok] = ktc_ref[0]
    kt_s[:, Q_TOK + step_tok:] = ktn_ref[0]
    ku = pltpu.bitcast(kt_s[...], jnp.uint32)
    ktsh_s[...] = pltpu.bitcast(pltpu.roll(ku, win_tok - GRID_W, axis=1), BF16)
    for a in range(HEADS_PER_GROUP):
        def own(vv):
            ln = lax.broadcasted_iota(jnp.int32, vv.shape, 1)
            keep = (ln >= a * HEAD_DIM) & (ln < (a + 1) * HEAD_DIM)
            return jnp.where(keep, vv, jnp.ones_like(vv))
        v_s[a, 0:Q_TOK, :] = own(vp_ref[0, 0])
        v_s[a, Q_TOK:Q_TOK + step_tok, :] = own(vc_ref[0, 0])
        v_s[a, Q_TOK + step_tok:win_tok, :] = own(vn_ref[0, 0])
        v_s[a, win_tok:, :] = own(vctx_ref[0, 0])
    for g in range(n_groups):
        qg = q_ref[0, 0, g * Q_TOK:(g + 1) * Q_TOK, :]
        for a in range(HEADS_PER_GROUP):
            in_head = (lane >= a * HEAD_DIM) & (lane < (a + 1) * HEAD_DIM)
            qm_s[g, a * Q_TOK:(a + 1) * Q_TOK, :] = jnp.where(in_head, qg, jnp.zeros_like(qg))

    def row_kind(j):
        if j < Q_ROWS:
            return jnp.where(s == 0, j, Q_ROWS)
        if j >= ROWS_PER_STEP - Q_ROWS:
            return jnp.where(s == steps_per_seq - 1, j - ROWS_PER_STEP + 2 * Q_ROWS + 1, Q_ROWS)
        return Q_ROWS

    def scores(g):
        sx = _dot(qm_s[g], kct_ref[0])
        s_s[g, :, BAND:] = sx
        mx = jnp.max(sx, axis=-1, keepdims=True)
        for jj in range(Q_ROWS):
            j = g * Q_ROWS + jj
            rows = [slice(a * Q_TOK + jj * GRID_W, a * Q_TOK + (jj + 1) * GRID_W)
                    for a in range(HEADS_PER_GROUP)]
            lhs = jnp.concatenate([qm_s[g, rs, :] for rs in rows], axis=0)
            if j % 2 == 0:
                band = kt_s[:, j * GRID_W:j * GRID_W + BAND]
            else:
                band = ktsh_s[:, (j - 1) * GRID_W:(j - 1) * GRID_W + BAND]
            kind = row_kind(j)
            bias = jnp.concatenate([bias_ref[kind, 0, a] for a in range(HEADS_PER_GROUP)], axis=0)
            sl = _dot(lhs, band) + bias
            ml = jnp.max(sl, axis=-1, keepdims=True)
            for a, rs in enumerate(rows):
                s_s[g, rs, 0:BAND] = sl[a * GRID_W:(a + 1) * GRID_W]
                m = jnp.maximum(ml[a * GRID_W:(a + 1) * GRID_W], mx[rs])
                m_s[g, rs, :] = jnp.broadcast_to(m, (GRID_W, LANES))

    def softmax(g, a):
        for r0 in range(a * Q_TOK, (a + 1) * Q_TOK, STRIP):
            m = m_s[g, r0:r0 + STRIP, :]
            for c0 in range(0, BAND + ctx_len, LANES):
                p_s[g, r0:r0 + STRIP, c0:c0 + LANES] = jnp.exp2(
                    s_s[g, r0:r0 + STRIP, c0:c0 + LANES] - m).astype(BF16)

    def values(g, a):
        acc = _dot(p_s[g, a * Q_TOK:(a + 1) * Q_TOK, BAND:], v_s[a, win_tok:, :])
        lat = []
        for jj in range(Q_ROWS):
            j = g * Q_ROWS + jj
            r0 = a * Q_TOK + jj * GRID_W
            lat.append(_dot(p_s[g, r0:r0 + GRID_W, 0:BAND], v_s[a, j * GRID_W:j * GRID_W + BAND, :]))
        acc = acc + jnp.concatenate(lat, axis=0)
        return acc / pltpu.roll(acc, HEAD_DIM, axis=1)

    scores(0)
    for g in range(n_groups):
        if g + 1 < n_groups:
            scores(g + 1)
        outs = []
        for a in range(HEADS_PER_GROUP):
            softmax(g, a)
            outs.append(values(g, a))
        o = jnp.where(lane < HEAD_DIM, outs[0], outs[1])
        o_ref[0, 0, g * Q_TOK:(g + 1) * Q_TOK, :] = o.astype(BF16)


def _attn_call(q, kt, v, kct, vctx, bias):
    n_seq, n_pairs, seq_len, _ = q.shape
    ctx_len = kct.shape[-1]
    step_tok = ROWS_PER_STEP * GRID_W
    win_tok = step_tok + 2 * Q_TOK
    steps = seq_len // step_tok
    n_tiles = seq_len // Q_TOK
    n_groups = ROWS_PER_STEP // Q_ROWS
    prev_t = lambda s: jnp.where(s == 0, 1, s * n_groups - 1)
    next_t = lambda s: jnp.where(s == steps - 1, n_tiles - 2, (s + 1) * n_groups)
    return pl.pallas_call(
        functools.partial(_attn_kernel, steps_per_seq=steps),
        grid=(n_pairs, n_seq, steps),
        in_specs=[
            pl.BlockSpec((1, 1, step_tok, LANES), lambda p, b, s: (b, p, s, 0)),
            pl.BlockSpec((1, LANES, Q_TOK), lambda p, b, s: (b, p, prev_t(s))),
            pl.BlockSpec((1, LANES, step_tok), lambda p, b, s: (b, p, s)),
            pl.BlockSpec((1, LANES, Q_TOK), lambda p, b, s: (b, p, next_t(s))),
            pl.BlockSpec((1, 1, Q_TOK, LANES), lambda p, b, s: (b, p, prev_t(s), 0)),
            pl.BlockSpec((1, 1, step_tok, LANES), lambda p, b, s: (b, p, s, 0)),
            pl.BlockSpec((1, 1, Q_TOK, LANES), lambda p, b, s: (b, p, next_t(s), 0)),
            pl.BlockSpec((1, LANES, ctx_len), lambda p, b, s: (b, p, 0)),
            pl.BlockSpec((1, 1, ctx_len, LANES), lambda p, b, s: (b, p, 0, 0)),
            pl.BlockSpec((2 * Q_ROWS + 1, 1, HEADS_PER_GROUP, GRID_W, BAND),
                         lambda p, b, s: (0, p, 0, 0, 0)),
        ],
        out_specs=pl.BlockSpec((1, 1, step_tok, LANES), lambda p, b, s: (b, p, s, 0)),
        out_shape=jax.ShapeDtypeStruct(q.shape, BF16),
        scratch_shapes=[
            pltpu.VMEM((LANES, win_tok), BF16),
            pltpu.VMEM((LANES, win_tok), BF16),
            pltpu.VMEM((HEADS_PER_GROUP, win_tok + ctx_len, LANES), BF16),
            pltpu.VMEM((n_groups, HEADS_PER_GROUP * Q_TOK, LANES), BF16),
            pltpu.VMEM((n_groups, HEADS_PER_GROUP * Q_TOK, BAND + ctx_len), F32),
            pltpu.VMEM((n_groups, HEADS_PER_GROUP * Q_TOK, LANES), F32),
            pltpu.VMEM((n_groups, HEADS_PER_GROUP * Q_TOK, BAND + ctx_len), BF16),
        ],
        compiler_params=_params(3),
        name="na_attn",
    )(q, kt, kt, kt, v, v, v, kct, vctx, bias)


def _out_kernel(o_ref, z_ref, x_ref, mod_ref, gpost_ref, w_ref, out_ref):
    tm, d = x_ref.shape
    n_groups = d // LANES
    o = jnp.concatenate([o_ref[0, p] for p in range(n_groups)], axis=-1).astype(F32)
    yv = (_silu(z_ref[...].astype(F32)) * o).astype(BF16)
    y = _dot(yv, w_ref[...])
    gt = mod_ref[0][:, 2 * d:]
    out_ref[...] = x_ref[...] + gt * (_rms(y) * gpost_ref[...])


def _out_call(o, z, x2, mod, g_post, w_out, *, seq_len, tm):
    n_tok, d = x2.shape
    tps = seq_len // tm
    n_seq = n_tok // seq_len
    n_groups = d // LANES
    const = lambda b, t: (0, 0)
    return pl.pallas_call(
        _out_kernel,
        grid=(n_seq, tps),
        in_specs=[
            pl.BlockSpec((1, n_groups, tm, LANES), lambda b, t: (b, 0, t, 0)),
            pl.BlockSpec((tm, d), lambda b, t: (b * tps + t, 0)),
            pl.BlockSpec((tm, d), lambda b, t: (b * tps + t, 0)),
            pl.BlockSpec((1, 1, 3 * d), lambda b, t: (b, 0, 0)),
            pl.BlockSpec((1, d), const),
            _resident((d, d), const),
        ],
        out_specs=pl.BlockSpec((tm, d), lambda b, t: (b * tps + t, 0)),
        out_shape=jax.ShapeDtypeStruct((n_tok, d), F32),
        compiler_params=_params(2),
        name="na_out",
    )(o, z, x2, mod, g_post, w_out)


def kernel(x, c, ctx, c_ctx, g_pre, g_post, w_mod, b_mod, w_in_conv, conv_w, w_out_conv,
           w_in_na, rpb, w_out_na):
    bsz, seq_len, d = x.shape
    _, ctx_len, _ = ctx.shape
    rows = seq_len // GRID_W
    assert g_pre.shape[0] == 2 and d == N_HEADS * HEAD_DIM
    assert rows % ROWS_PER_STEP == 0 and ROWS_PER_STEP >= 2 * Q_ROWS and rows >= 3 * Q_ROWS

    mod_rows = -(-(bsz + 1) // HALO) * HALO
    cvec = jnp.zeros((mod_rows, d), F32).at[:bsz].set(c).at[bsz].set(c_ctx)
    mods = _mod_call(cvec, w_mod, b_mod)
    mod_lat = [mods[i, :bsz].reshape(bsz, 1, 3 * d) for i in range(2)]
    mod_ctx = [mods[i, bsz:bsz + 1].reshape(1, 1, 3 * d) for i in range(2)]

    gp = [g_pre[i].reshape(1, d) for i in range(2)]
    gq = [g_post[i].reshape(1, d) for i in range(2)]

    w_in0 = w_in_conv[0].astype(BF16)
    w_out0 = w_out_conv[0].astype(BF16)
    x2 = x.reshape(bsz * seq_len, d)
    c2 = ctx.reshape(bsz * ctx_len, d)
    x1 = _conv_call(x2, mod_lat[0], gp[0], gq[0], w_in0, conv_w[0], w_out0,
                    seq_len=seq_len, tm=512, mod_per_seq=True)
    c1 = _conv_call(c2, mod_ctx[0], gp[0], gq[0], w_in0, conv_w[0], w_out0,
                    seq_len=ctx_len, tm=ctx_len, mod_per_seq=False)

    wi = w_in_na[0]
    wq = wi[:, :d].astype(BF16)
    wkt = wi[:, d:2 * d].T.astype(BF16)
    wv = wi[:, 2 * d:3 * d].astype(BF16)
    wz = wi[:, 3 * d:].astype(BF16)
    q, kt, v, z = _proj_call(x1, mod_lat[1], gp[1], wq, wkt, wv, wz,
                             n_seq=bsz, seq_len=seq_len, tm=512, mod_per_seq=True)
    _, kct, vctx, _ = _proj_call(c1, mod_ctx[1], gp[1], wq, wkt, wv, wz,
                                 n_seq=bsz, seq_len=ctx_len, tm=ctx_len, mod_per_seq=False)
    o = _attn_call(q, kt, v, kct, vctx, _bias_tables(rpb[0]))
    out = _out_call(o, z, x1, mod_lat[1], gq[1], w_out_na[0].astype(BF16), seq_len=seq_len, tm=1024)
    return out.reshape(bsz, seq_len, d)
```

```python
import functools

import jax
import jax.numpy as jnp
import numpy as np
from jax import lax
from jax.experimental import pallas as pl
from jax.experimental.pallas import tpu as pltpu

F32 = jnp.float32
BF16 = jnp.bfloat16

GRID_W = 64
N_HEADS = 16
HEAD_DIM = 64
WIN_R = 8
WIN_C = 16
EPS = 1e-6
NEG = -1e30

LANES = 128
HEADS_PER_GROUP = LANES // HEAD_DIM
HALO = 8
Q_ROWS = 4
Q_TOK = Q_ROWS * GRID_W
BAND = WIN_R * GRID_W
ROWS_PER_STEP = 32
STRIP = 16
LOG2E = 1.4426950408889634
VMEM_LIMIT = 56 * 1024 * 1024


def _resident(shape, index_map):
    return pl.BlockSpec(shape, index_map, pipeline_mode=pl.Buffered(1))


def _params(n_axes):
    return pltpu.CompilerParams(dimension_semantics=("arbitrary",) * n_axes,
                                vmem_limit_bytes=VMEM_LIMIT)


def _silu(v):
    return v * jax.nn.sigmoid(v)


def _rms(v):
    return v * lax.rsqrt(jnp.mean(v * v, axis=-1, keepdims=True) + EPS)


def _dot(a, b):
    return jnp.dot(a, b, preferred_element_type=F32)


def _mod_kernel(c_ref, w_ref, b_ref, o_ref):
    s = _silu(c_ref[...])
    o_ref[0] = jnp.dot(s, w_ref[0], preferred_element_type=F32,
                       precision=lax.Precision.HIGHEST) + b_ref[0]


def _mod_call(cvec, w_mod, b_mod):
    depth, d, d3 = w_mod.shape
    rows = cvec.shape[0]
    tn = 1024
    return pl.pallas_call(
        _mod_kernel,
        grid=(depth, d3 // tn),
        in_specs=[
            pl.BlockSpec((rows, d), lambda i, n: (0, 0)),
            pl.BlockSpec((1, d, tn), lambda i, n: (i, 0, n)),
            pl.BlockSpec((1, 1, tn), lambda i, n: (i, 0, n)),
        ],
        out_specs=pl.BlockSpec((1, rows, tn), lambda i, n: (i, 0, n)),
        out_shape=jax.ShapeDtypeStruct((depth, rows, d3), F32),
        compiler_params=_params(2),
        name="mod",
    )(cvec, w_mod, b_mod.reshape(depth, 1, d3))


def _norm_mod(xv, g, mod, d):
    sh = mod[:, :d]
    sc = mod[:, d:2 * d]
    return _rms(xv) * g * (1.0 + sc) + sh


def _front_kernel(x_ref, xp_ref, xn_ref, mod0_ref, mod1_ref, gpre0_ref, gpost0_ref, gpre1_ref,
                  win_ref, cw_ref, wout_ref, wq_ref, wkt_ref, wv_ref, wz_ref,
                  x1_ref, q_ref, kt_ref, v_ref, z_ref, yv_ref, *, cb):
    tm, d = x_ref.shape
    n_pairs = d // LANES
    ti = pl.program_id(1)
    mod0 = mod0_ref[0]
    g0 = gpre0_ref[...]
    hm = _norm_mod(x_ref[...], g0, mod0, d)
    keep_p = (ti > 0).astype(F32)
    keep_n = (ti < pl.num_programs(1) - 1).astype(F32)
    hp = _norm_mod(xp_ref[...], g0, mod0, d)
    hn = _norm_mod(xn_ref[...], g0, mod0, d)
    hcat = jnp.concatenate([hp, hm, hn], axis=0).astype(BF16)
    hmb = hm.astype(BF16)
    rid = lax.broadcasted_iota(jnp.int32, (tm + 2 * HALO, 1), 0)
    rowkeep = jnp.where(rid < HALO, keep_p, jnp.where(rid >= tm + HALO, keep_n, 1.0))
    for nb in range(d // cb):
        c0 = nb * cb
        cg = _dot(hcat, win_ref[:, d + c0:d + c0 + cb])
        u = _dot(hcat, win_ref[:, 2 * d + c0:2 * d + c0 + cb])
        cu = cg * u * rowkeep
        w = cw_ref[:, c0:c0 + cb]
        conv = (w[0:1] * cu[HALO - 1:HALO - 1 + tm]
                + w[1:2] * cu[HALO:HALO + tm]
                + w[2:3] * cu[HALO + 1:HALO + 1 + tm])
        bg = _dot(hmb, win_ref[:, c0:c0 + cb])
        z0 = _dot(hmb, win_ref[:, 3 * d + c0:3 * d + c0 + cb])
        yv_ref[:, c0:c0 + cb] = (bg * conv * _silu(z0)).astype(BF16)
    half = tm // 2
    halves = (0, half)
    ys = [_dot(yv_ref[r0:r0 + half, :], wout_ref[...]) for r0 in halves]
    gt0 = mod0[:, 2 * d:]
    mod1 = mod1_ref[0]
    for r0, y in zip(halves, ys):
        x1 = x_ref[r0:r0 + half, :] + gt0 * (_rms(y) * gpost0_ref[...])
        x1_ref[r0:r0 + half, :] = x1
        hb = _norm_mod(x1, gpre1_ref[...], mod1, d).astype(BF16)
        q = _dot(hb, wq_ref[...]) * (HEAD_DIM ** -0.5 * LOG2E)
        for p in range(n_pairs):
            q_ref[0, p, r0:r0 + half, :] = q[:, p * LANES:(p + 1) * LANES].astype(BF16)
        v = _dot(hb, wv_ref[...])
        for p in range(n_pairs):
            v_ref[0, p, r0:r0 + half, :] = v[:, p * LANES:(p + 1) * LANES].astype(BF16)
        z_ref[r0:r0 + half, :] = _dot(hb, wz_ref[...]).astype(BF16)
        kt = lax.dot_general(wkt_ref[...], hb, (((1,), (1,)), ((), ())), preferred_element_type=F32)
        kt_ref[0, :, r0:r0 + half] = kt.astype(BF16)


def _front_call(x2, mod0, mod1, g_pre0, g_post0, g_pre1, w_in, conv_w, w_out, wq, wkt, wv, wz,
                *, n_seq, seq_len, tm, mod_per_seq):
    n_tok, d = x2.shape
    tps = seq_len // tm
    n_pairs = d // LANES
    hb = tm // HALO
    last_hb = n_tok // HALO - 1
    tile = lambda b, t: b * tps + t
    if mod_per_seq:
        mod_map = lambda b, t: (b, 0, 0)
    else:
        mod_map = lambda b, t: (0, 0, 0)
    const = lambda b, t: (0, 0)
    paired = jax.ShapeDtypeStruct((n_seq, n_pairs, seq_len, LANES), BF16)
    return pl.pallas_call(
        functools.partial(_front_kernel, cb=256),
        grid=(n_seq, tps),
        in_specs=[
            pl.BlockSpec((tm, d), lambda b, t: (tile(b, t), 0)),
            pl.BlockSpec((HALO, d), lambda b, t: (jnp.maximum(tile(b, t) * hb - 1, 0), 0)),
            pl.BlockSpec((HALO, d), lambda b, t: (jnp.minimum((tile(b, t) + 1) * hb, last_hb), 0)),
            pl.BlockSpec((1, 1, 3 * d), mod_map),
            pl.BlockSpec((1, 1, 3 * d), mod_map),
            pl.BlockSpec((1, d), const),
            pl.BlockSpec((1, d), const),
            pl.BlockSpec((1, d), const),
            _resident((d, 4 * d), const),
            pl.BlockSpec((3, d), const),
            _resident((d, d), const),
            _resident((d, d), const),
            _resident((d, d), const),
            _resident((d, d), const),
            _resident((d, d), const),
        ],
        out_specs=[
            pl.BlockSpec((tm, d), lambda b, t: (tile(b, t), 0)),
            pl.BlockSpec((1, n_pairs, tm, LANES), lambda b, t: (b, 0, t, 0)),
            pl.BlockSpec((1, d, tm), lambda b, t: (b, 0, t)),
            pl.BlockSpec((1, n_pairs, tm, LANES), lambda b, t: (b, 0, t, 0)),
            pl.BlockSpec((tm, d), lambda b, t: (tile(b, t), 0)),
        ],
        out_shape=[
            jax.ShapeDtypeStruct((n_tok, d), F32),
            paired,
            jax.ShapeDtypeStruct((n_seq, d, seq_len), BF16),
            paired,
            jax.ShapeDtypeStruct((n_tok, d), BF16),
        ],
        scratch_shapes=[pltpu.VMEM((tm, d), BF16)],
        compiler_params=_params(2),
        name="front",
    )(x2, x2, x2, mod0, mod1, g_pre0, g_post0, g_pre1, w_in, conv_w, w_out, wq, wkt, wv, wz)


def _band_rows():
    kinds = []
    for t in range(2 * Q_ROWS + 1):
        drs = []
        for i in range(WIN_R):
            if t < Q_ROWS:
                p = t + i
                key_minus_query = (p + Q_ROWS if p < Q_ROWS else p - Q_ROWS) - t
            elif t == Q_ROWS:
                key_minus_query = i - Q_ROWS
            else:
                jj = t - Q_ROWS - 1
                key_minus_query = i - Q_ROWS if i < WIN_R - jj else i - Q_ROWS - WIN_R
            drs.append(key_minus_query + WIN_R - 1)
        kinds.append(drs)
    return kinds


def _bias_kernel(rp_ref, o_ref):
    qc = lax.broadcasted_iota(jnp.int32, (GRID_W, GRID_W), 0)
    kc = lax.broadcasted_iota(jnp.int32, (GRID_W, GRID_W), 1)
    cs = jnp.clip(qc - WIN_C // 2, 0, GRID_W - WIN_C)
    col_ok = (kc >= cs) & (kc < cs + WIN_C)
    blocks = []
    for dr in range(2 * WIN_R - 1):
        row = jnp.broadcast_to(rp_ref[0, dr:dr + 1, :], (GRID_W, LANES))
        toep = pltpu.roll(row, GRID_W, axis=1, stride=1, stride_axis=0)[:, :GRID_W]
        blocks.append(jnp.where(col_ok, toep * LOG2E, NEG))
    for t, drs in enumerate(_band_rows()):
        o_ref[t, 0, 0] = jnp.concatenate([blocks[dr] for dr in drs], axis=1)


def _bias_tables(rpb):
    h, n_dr, n_dc = rpb.shape
    lane0 = GRID_W - WIN_C + 1
    rp = jnp.zeros((h, 2 * HALO, LANES), F32).at[:, :n_dr, lane0:lane0 + n_dc].set(rpb)
    n_kinds = 2 * Q_ROWS + 1
    return pl.pallas_call(
        _bias_kernel,
        grid=(h,),
        in_specs=[pl.BlockSpec((1, 2 * HALO, LANES), lambda i: (i, 0, 0))],
        out_specs=pl.BlockSpec((n_kinds, 1, 1, GRID_W, BAND),
                               lambda i: (0, i // HEADS_PER_GROUP, i % HEADS_PER_GROUP, 0, 0)),
        out_shape=jax.ShapeDtypeStruct((n_kinds, h // HEADS_PER_GROUP, HEADS_PER_GROUP, GRID_W, BAND), F32),
        compiler_params=_params(1),
        name="bias_tables",
    )(rp)


def _attn_kernel(q_ref, ktp_ref, ktc_ref, ktn_ref, vp_ref, vc_ref, vn_ref, kct_ref, vctx_ref,
                 bias_ref, o_ref, kt_s, ktsh_s, v_s, qm_s, s_s, p_s, *, steps_per_seq):
    s = pl.program_id(2)
    step_tok = ROWS_PER_STEP * GRID_W
    win_tok = step_tok + 2 * Q_TOK
    n_groups = ROWS_PER_STEP // Q_ROWS
    lane = lax.broadcasted_iota(jnp.int32, (Q_TOK, LANES), 1)

    kt_s[:, 0:Q_TOK] = ktp_ref[0]
    kt_s[:, Q_TOK:Q_TOK + step_tok] = ktc_ref[0]
    kt_s[:, Q_TOK + step_tok:] = ktn_ref[0]
    ku = pltpu.bitcast(kt_s[...], jnp.uint32)
    ktsh_s[...] = pltpu.bitcast(pltpu.roll(ku, win_tok - GRID_W, axis=1), BF16)
    for a in range(HEADS_PER_GROUP):
        def own(vv):
            ln = lax.broadcasted_iota(jnp.int32, vv.shape, 1)
            keep = (ln >= a * HEAD_DIM) & (ln < (a + 1) * HEAD_DIM)
            return jnp.where(keep, vv, jnp.ones_like(vv))
        v_s[a, 0:Q_TOK, :] = own(vp_ref[0, 0])
        v_s[a, Q_TOK:Q_TOK + step_tok, :] = own(vc_ref[0, 0])
        v_s[a, Q_TOK + step_tok:win_tok, :] = own(vn_ref[0, 0])
        v_s[a, win_tok:, :] = own(vctx_ref[0, 0])
    for g in range(n_groups):
        qg = q_ref[0, 0, g * Q_TOK:(g + 1) * Q_TOK, :]
        for a in range(HEADS_PER_GROUP):
            in_head = (lane >= a * HEAD_DIM) & (lane < (a + 1) * HEAD_DIM)
            qm_s[g, a * Q_TOK:(a + 1) * Q_TOK, :] = jnp.where(in_head, qg, jnp.zeros_like(qg))

    def scores(g):
        for jj in range(Q_ROWS):
            j = g * Q_ROWS + jj
            lhs = jnp.concatenate([qm_s[g, a * Q_TOK + jj * GRID_W:a * Q_TOK + (jj + 1) * GRID_W, :]
                                   for a in range(HEADS_PER_GROUP)], axis=0)
            if j % 2 == 0:
                band = kt_s[:, j * GRID_W:j * GRID_W + BAND]
            else:
                band = ktsh_s[:, (j - 1) * GRID_W:(j - 1) * GRID_W + BAND]
            r = _dot(lhs, band)
            for a in range(HEADS_PER_GROUP):
                s_s[g, a * Q_TOK + jj * GRID_W:a * Q_TOK + (jj + 1) * GRID_W, 0:BAND] = (
                    r[a * GRID_W:(a + 1) * GRID_W])
        s_s[g, :, BAND:] = _dot(qm_s[g], kct_ref[0])

    def softmax(g):
        for a in range(HEADS_PER_GROUP):
            for jj in range(Q_ROWS):
                j = g * Q_ROWS + jj
                if j < Q_ROWS:
                    kind = jnp.where(s == 0, j, Q_ROWS)
                elif j >= ROWS_PER_STEP - Q_ROWS:
                    kind = jnp.where(s == steps_per_seq - 1, j - ROWS_PER_STEP + 2 * Q_ROWS + 1, Q_ROWS)
                else:
                    kind = Q_ROWS
                for t0 in range(0, GRID_W, STRIP):
                    r0 = a * Q_TOK + jj * GRID_W + t0
                    sl = s_s[g, r0:r0 + STRIP, 0:BAND] + bias_ref[kind, 0, a, t0:t0 + STRIP, :]
                    sx = s_s[g, r0:r0 + STRIP, BAND:]
                    m = jnp.maximum(jnp.max(sl, axis=-1, keepdims=True),
                                    jnp.max(sx, axis=-1, keepdims=True))
                    p_s[g, r0:r0 + STRIP, 0:BAND] = jnp.exp2(sl - m).astype(BF16)
                    p_s[g, r0:r0 + STRIP, BAND:] = jnp.exp2(sx - m).astype(BF16)

    def values(g):
        outs = []
        for a in range(HEADS_PER_GROUP):
            acc = _dot(p_s[g, a * Q_TOK:(a + 1) * Q_TOK, BAND:], v_s[a, win_tok:, :])
            lat = []
            for jj in range(Q_ROWS):
                j = g * Q_ROWS + jj
                r0 = a * Q_TOK + jj * GRID_W
                lat.append(_dot(p_s[g, r0:r0 + GRID_W, 0:BAND], v_s[a, j * GRID_W:j * GRID_W + BAND, :]))
            acc = acc + jnp.concatenate(lat, axis=0)
            outs.append(acc / pltpu.roll(acc, HEAD_DIM, axis=1))
        o = jnp.where(lane < HEAD_DIM, outs[0], outs[1])
        o_ref[0, 0, g * Q_TOK:(g + 1) * Q_TOK, :] = o.astype(BF16)

    scores(0)
    for g in range(n_groups):
        if g + 1 < n_groups:
            scores(g + 1)
        softmax(g)
        values(g)


def _attn_call(q, kt, v, kct, vctx, bias):
    n_seq, n_pairs, seq_len, _ = q.shape
    ctx_len = kct.shape[-1]
    step_tok = ROWS_PER_STEP * GRID_W
    win_tok = step_tok + 2 * Q_TOK
    steps = seq_len // step_tok
    n_tiles = seq_len // Q_TOK
    n_groups = ROWS_PER_STEP // Q_ROWS
    prev_t = lambda s: jnp.where(s == 0, 1, s * n_groups - 1)
    next_t = lambda s: jnp.where(s == steps - 1, n_tiles - 2, (s + 1) * n_groups)
    return pl.pallas_call(
        functools.partial(_attn_kernel, steps_per_seq=steps),
        grid=(n_pairs, n_seq, steps),
        in_specs=[
            pl.BlockSpec((1, 1, step_tok, LANES), lambda p, b, s: (b, p, s, 0)),
            pl.BlockSpec((1, LANES, Q_TOK), lambda p, b, s: (b, p, prev_t(s))),
            pl.BlockSpec((1, LANES, step_tok), lambda p, b, s: (b, p, s)),
            pl.BlockSpec((1, LANES, Q_TOK), lambda p, b, s: (b, p, next_t(s))),
            pl.BlockSpec((1, 1, Q_TOK, LANES), lambda p, b, s: (b, p, prev_t(s), 0)),
            pl.BlockSpec((1, 1, step_tok, LANES), lambda p, b, s: (b, p, s, 0)),
            pl.BlockSpec((1, 1, Q_TOK, LANES), lambda p, b, s: (b, p, next_t(s), 0)),
            pl.BlockSpec((1, LANES, ctx_len), lambda p, b, s: (b, p, 0)),
            pl.BlockSpec((1, 1, ctx_len, LANES), lambda p, b, s: (b, p, 0, 0)),
            pl.BlockSpec((2 * Q_ROWS + 1, 1, HEADS_PER_GROUP, GRID_W, BAND),
                         lambda p, b, s: (0, p, 0, 0, 0)),
        ],
        out_specs=pl.BlockSpec((1, 1, step_tok, LANES), lambda p, b, s: (b, p, s, 0)),
        out_shape=jax.ShapeDtypeStruct(q.shape, BF16),
        scratch_shapes=[
            pltpu.VMEM((LANES, win_tok), BF16),
            pltpu.VMEM((LANES, win_tok), BF16),
            pltpu.VMEM((HEADS_PER_GROUP, win_tok + ctx_len, LANES), BF16),
            pltpu.VMEM((n_groups, HEADS_PER_GROUP * Q_TOK, LANES), BF16),
            pltpu.VMEM((n_groups, HEADS_PER_GROUP * Q_TOK, BAND + ctx_len), F32),
            pltpu.VMEM((n_groups, HEADS_PER_GROUP * Q_TOK, BAND + ctx_len), BF16),
        ],
        compiler_params=_params(3),
        name="na_attn",
    )(q, kt, kt, kt, v, v, v, kct, vctx, bias)


def _out_kernel(o_ref, z_ref, x_ref, mod_ref, gpost_ref, w_ref, out_ref):
    tm, d = x_ref.shape
    n_groups = d // LANES
    o = jnp.concatenate([o_ref[0, p] for p in range(n_groups)], axis=-1).astype(F32)
    yv = (_silu(z_ref[...].astype(F32)) * o).astype(BF16)
    y = _dot(yv, w_ref[...])
    gt = mod_ref[0][:, 2 * d:]
    out_ref[...] = x_ref[...] + gt * (_rms(y) * gpost_ref[...])


def _out_call(o, z, x2, mod, g_post, w_out, *, seq_len, tm):
    n_tok, d = x2.shape
    tps = seq_len // tm
    n_seq = n_tok // seq_len
    n_groups = d // LANES
    const = lambda b, t: (0, 0)
    return pl.pallas_call(
        _out_kernel,
        grid=(n_seq, tps),
        in_specs=[
            pl.BlockSpec((1, n_groups, tm, LANES), lambda b, t: (b, 0, t, 0)),
            pl.BlockSpec((tm, d), lambda b, t: (b * tps + t, 0)),
            pl.BlockSpec((tm, d), lambda b, t: (b * tps + t, 0)),
            pl.BlockSpec((1, 1, 3 * d), lambda b, t: (b, 0, 0)),
            pl.BlockSpec((1, d), const),
            _resident((d, d), const),
        ],
        out_specs=pl.BlockSpec((tm, d), lambda b, t: (b * tps + t, 0)),
        out_shape=jax.ShapeDtypeStruct((n_tok, d), F32),
        compiler_params=_params(2),
        name="na_out",
    )(o, z, x2, mod, g_post, w_out)


def kernel(x, c, ctx, c_ctx, g_pre, g_post, w_mod, b_mod, w_in_conv, conv_w, w_out_conv,
           w_in_na, rpb, w_out_na):
    bsz, seq_len, d = x.shape
    _, ctx_len, _ = ctx.shape
    rows = seq_len // GRID_W
    assert g_pre.shape[0] == 2 and d == N_HEADS * HEAD_DIM
    assert rows % ROWS_PER_STEP == 0 and ROWS_PER_STEP >= 2 * Q_ROWS and rows >= 3 * Q_ROWS

    mod_rows = -(-(bsz + 1) // HALO) * HALO
    cvec = jnp.zeros((mod_rows, d), F32).at[:bsz].set(c).at[bsz].set(c_ctx)
    mods = _mod_call(cvec, w_mod, b_mod)
    mod_lat = [mods[i, :bsz].reshape(bsz, 1, 3 * d) for i in range(2)]
    mod_ctx = [mods[i, bsz:bsz + 1].reshape(1, 1, 3 * d) for i in range(2)]

    gp = [g_pre[i].reshape(1, d) for i in range(2)]
    gq = [g_post[i].reshape(1, d) for i in range(2)]

    w_in0 = w_in_conv[0].astype(BF16)
    w_out0 = w_out_conv[0].astype(BF16)
    wi = w_in_na[0]
    wq = wi[:, :d].astype(BF16)
    wkt = wi[:, d:2 * d].T.astype(BF16)
    wv = wi[:, 2 * d:3 * d].astype(BF16)
    wz = wi[:, 3 * d:].astype(BF16)
    weights = (gp[0], gq[0], gp[1], w_in0, conv_w[0], w_out0, wq, wkt, wv, wz)
    x1, q, kt, v, z = _front_call(x.reshape(bsz * seq_len, d), mod_lat[0], mod_lat[1], *weights,
                                  n_seq=bsz, seq_len=seq_len, tm=512, mod_per_seq=True)
    _, _, kct, vctx, _ = _front_call(ctx.reshape(bsz * ctx_len, d), mod_ctx[0], mod_ctx[1], *weights,
                                     n_seq=bsz, seq_len=ctx_len, tm=ctx_len, mod_per_seq=False)
    o = _attn_call(q, kt, v, kct, vctx, _bias_tables(rpb[0]))
    out = _out_call(o, z, x1, mod_lat[1], gq[1], w_out_na[0].astype(BF16), seq_len=seq_len, tm=1024)
    return out.reshape(bsz, seq_len, d)
```

```python
import functools

import jax
import jax.numpy as jnp
import numpy as np
from jax import lax
from jax.experimental import pallas as pl
from jax.experimental.pallas import tpu as pltpu

F32 = jnp.float32
BF16 = jnp.bfloat16

GRID_W = 64
N_HEADS = 16
HEAD_DIM = 64
WIN_R = 8
WIN_C = 16
EPS = 1e-6
NEG = -1e30

LANES = 128
HEADS_PER_GROUP = LANES // HEAD_DIM
HALO = 8
Q_ROWS = 4
Q_TOK = Q_ROWS * GRID_W
BAND = WIN_R * GRID_W
ROWS_PER_STEP = 64
SCORE_SLOTS = 3
STRIP = 16
LOG2E = 1.4426950408889634
VMEM_LIMIT = 56 * 1024 * 1024


def _resident(shape, index_map):
    return pl.BlockSpec(shape, index_map, pipeline_mode=pl.Buffered(1))


def _params(n_axes):
    return pltpu.CompilerParams(dimension_semantics=("arbitrary",) * n_axes,
                                vmem_limit_bytes=VMEM_LIMIT)


def _silu(v):
    return v * jax.nn.sigmoid(v)


def _rms(v):
    return v * lax.rsqrt(jnp.mean(v * v, axis=-1, keepdims=True) + EPS)


def _dot(a, b):
    return jnp.dot(a, b, preferred_element_type=F32)


def _mod_kernel(c_ref, w_ref, b_ref, o_ref):
    s = _silu(c_ref[...])
    o_ref[0] = jnp.dot(s, w_ref[0], preferred_element_type=F32,
                       precision=lax.Precision.HIGHEST) + b_ref[0]


def _mod_call(cvec, w_mod, b_mod):
    depth, d, d3 = w_mod.shape
    rows = cvec.shape[0]
    tn = 1024
    return pl.pallas_call(
        _mod_kernel,
        grid=(depth, d3 // tn),
        in_specs=[
            pl.BlockSpec((rows, d), lambda i, n: (0, 0)),
            pl.BlockSpec((1, d, tn), lambda i, n: (i, 0, n)),
            pl.BlockSpec((1, 1, tn), lambda i, n: (i, 0, n)),
        ],
        out_specs=pl.BlockSpec((1, rows, tn), lambda i, n: (i, 0, n)),
        out_shape=jax.ShapeDtypeStruct((depth, rows, d3), F32),
        compiler_params=_params(2),
        name="mod",
    )(cvec, w_mod, b_mod.reshape(depth, 1, d3))


def _norm_mod(xv, g, mod, d):
    sh = mod[:, :d]
    sc = mod[:, d:2 * d]
    return _rms(xv) * g * (1.0 + sc) + sh


def _front_kernel(x_ref, xp_ref, xn_ref, mod0_ref, mod1_ref, gpre0_ref, gpost0_ref, gpre1_ref,
                  win_ref, cw_ref, wout_ref, wq_ref, wkt_ref, wv_ref, wz_ref,
                  x1_ref, q_ref, kt_ref, v_ref, z_ref, yv_ref, *, cb):
    tm, d = x_ref.shape
    n_pairs = d // LANES
    ti = pl.program_id(1)
    mod0 = mod0_ref[0]
    g0 = gpre0_ref[...]
    hm = _norm_mod(x_ref[...], g0, mod0, d)
    keep_p = (ti > 0).astype(F32)
    keep_n = (ti < pl.num_programs(1) - 1).astype(F32)
    hp = _norm_mod(xp_ref[...], g0, mod0, d)
    hn = _norm_mod(xn_ref[...], g0, mod0, d)
    hcat = jnp.concatenate([hp, hm, hn], axis=0).astype(BF16)
    hmb = hm.astype(BF16)
    rid = lax.broadcasted_iota(jnp.int32, (tm + 2 * HALO, 1), 0)
    rowkeep = jnp.where(rid < HALO, keep_p, jnp.where(rid >= tm + HALO, keep_n, 1.0))
    for nb in range(d // cb):
        c0 = nb * cb
        cg = _dot(hcat, win_ref[:, d + c0:d + c0 + cb])
        u = _dot(hcat, win_ref[:, 2 * d + c0:2 * d + c0 + cb])
        cu = cg * u * rowkeep
        w = cw_ref[:, c0:c0 + cb]
        conv = (w[0:1] * cu[HALO - 1:HALO - 1 + tm]
                + w[1:2] * cu[HALO:HALO + tm]
                + w[2:3] * cu[HALO + 1:HALO + 1 + tm])
        bg = _dot(hmb, win_ref[:, c0:c0 + cb])
        z0 = _dot(hmb, win_ref[:, 3 * d + c0:3 * d + c0 + cb])
        yv_ref[:, c0:c0 + cb] = (bg * conv * _silu(z0)).astype(BF16)
    half = tm // 2
    halves = (0, half)
    ys = [_dot(yv_ref[r0:r0 + half, :], wout_ref[...]) for r0 in halves]
    gt0 = mod0[:, 2 * d:]
    mod1 = mod1_ref[0]
    for r0, y in zip(halves, ys):
        x1 = x_ref[r0:r0 + half, :] + gt0 * (_rms(y) * gpost0_ref[...])
        x1_ref[r0:r0 + half, :] = x1
        hb = _norm_mod(x1, gpre1_ref[...], mod1, d).astype(BF16)
        q = _dot(hb, wq_ref[...]) * (HEAD_DIM ** -0.5 * LOG2E)
        for p in range(n_pairs):
            q_ref[0, p, r0:r0 + half, :] = q[:, p * LANES:(p + 1) * LANES].astype(BF16)
        v = _dot(hb, wv_ref[...])
        for p in range(n_pairs):
            v_ref[0, p, r0:r0 + half, :] = v[:, p * LANES:(p + 1) * LANES].astype(BF16)
        z_ref[r0:r0 + half, :] = _dot(hb, wz_ref[...]).astype(BF16)
        kt = lax.dot_general(wkt_ref[...], hb, (((1,), (1,)), ((), ())), preferred_element_type=F32)
        kt_ref[0, :, r0:r0 + half] = kt.astype(BF16)


def _front_call(x2, mod0, mod1, g_pre0, g_post0, g_pre1, w_in, conv_w, w_out, wq, wkt, wv, wz,
                *, n_seq, seq_len, tm, mod_per_seq):
    n_tok, d = x2.shape
    tps = seq_len // tm
    n_pairs = d // LANES
    hb = tm // HALO
    last_hb = n_tok // HALO - 1
    tile = lambda b, t: b * tps + t
    if mod_per_seq:
        mod_map = lambda b, t: (b, 0, 0)
    else:
        mod_map = lambda b, t: (0, 0, 0)
    const = lambda b, t: (0, 0)
    paired = jax.ShapeDtypeStruct((n_seq, n_pairs, seq_len, LANES), BF16)
    return pl.pallas_call(
        functools.partial(_front_kernel, cb=256),
        grid=(n_seq, tps),
        in_specs=[
            pl.BlockSpec((tm, d), lambda b, t: (tile(b, t), 0)),
            pl.BlockSpec((HALO, d), lambda b, t: (jnp.maximum(tile(b, t) * hb - 1, 0), 0)),
            pl.BlockSpec((HALO, d), lambda b, t: (jnp.minimum((tile(b, t) + 1) * hb, last_hb), 0)),
            pl.BlockSpec((1, 1, 3 * d), mod_map),
            pl.BlockSpec((1, 1, 3 * d), mod_map),
            pl.BlockSpec((1, d), const),
            pl.BlockSpec((1, d), const),
            pl.BlockSpec((1, d), const),
            _resident((d, 4 * d), const),
            pl.BlockSpec((3, d), const),
            _resident((d, d), const),
            _resident((d, d), const),
            _resident((d, d), const),
            _resident((d, d), const),
            _resident((d, d), const),
        ],
        out_specs=[
            pl.BlockSpec((tm, d), lambda b, t: (tile(b, t), 0)),
            pl.BlockSpec((1, n_pairs, tm, LANES), lambda b, t: (b, 0, t, 0)),
            pl.BlockSpec((1, d, tm), lambda b, t: (b, 0, t)),
            pl.BlockSpec((1, n_pairs, tm, LANES), lambda b, t: (b, 0, t, 0)),
            pl.BlockSpec((tm, d), lambda b, t: (tile(b, t), 0)),
        ],
        out_shape=[
            jax.ShapeDtypeStruct((n_tok, d), F32),
            paired,
            jax.ShapeDtypeStruct((n_seq, d, seq_len), BF16),
            paired,
            jax.ShapeDtypeStruct((n_tok, d), BF16),
        ],
        scratch_shapes=[pltpu.VMEM((tm, d), BF16)],
        compiler_params=_params(2),
        name="front",
    )(x2, x2, x2, mod0, mod1, g_pre0, g_post0, g_pre1, w_in, conv_w, w_out, wq, wkt, wv, wz)


def _band_rows():
    kinds = []
    for t in range(2 * Q_ROWS + 1):
        drs = []
        for i in range(WIN_R):
            if t < Q_ROWS:
                p = t + i
                key_minus_query = (p + Q_ROWS if p < Q_ROWS else p - Q_ROWS) - t
            elif t == Q_ROWS:
                key_minus_query = i - Q_ROWS
            else:
                jj = t - Q_ROWS - 1
                key_minus_query = i - Q_ROWS if i < WIN_R - jj else i - Q_ROWS - WIN_R
            drs.append(key_minus_query + WIN_R - 1)
        kinds.append(drs)
    return kinds


def _bias_kernel(rp_ref, o_ref):
    qc = lax.broadcasted_iota(jnp.int32, (GRID_W, GRID_W), 0)
    kc = lax.broadcasted_iota(jnp.int32, (GRID_W, GRID_W), 1)
    cs = jnp.clip(qc - WIN_C // 2, 0, GRID_W - WIN_C)
    col_ok = (kc >= cs) & (kc < cs + WIN_C)
    blocks = []
    for dr in range(2 * WIN_R - 1):
        row = jnp.broadcast_to(rp_ref[0, dr:dr + 1, :], (GRID_W, LANES))
        toep = pltpu.roll(row, GRID_W, axis=1, stride=1, stride_axis=0)[:, :GRID_W]
        blocks.append(jnp.where(col_ok, toep * LOG2E, NEG))
    for t, drs in enumerate(_band_rows()):
        o_ref[t, 0, 0] = jnp.concatenate([blocks[dr] for dr in drs], axis=1)


def _bias_tables(rpb):
    h, n_dr, n_dc = rpb.shape
    lane0 = GRID_W - WIN_C + 1
    rp = jnp.zeros((h, 2 * HALO, LANES), F32).at[:, :n_dr, lane0:lane0 + n_dc].set(rpb)
    n_kinds = 2 * Q_ROWS + 1
    return pl.pallas_call(
        _bias_kernel,
        grid=(h,),
        in_specs=[pl.BlockSpec((1, 2 * HALO, LANES), lambda i: (i, 0, 0))],
        out_specs=pl.BlockSpec((n_kinds, 1, 1, GRID_W, BAND),
                               lambda i: (0, i // HEADS_PER_GROUP, i % HEADS_PER_GROUP, 0, 0)),
        out_shape=jax.ShapeDtypeStruct((n_kinds, h // HEADS_PER_GROUP, HEADS_PER_GROUP, GRID_W, BAND), F32),
        compiler_params=_params(1),
        name="bias_tables",
    )(rp)


def _attn_kernel(q_ref, ktp_ref, ktc_ref, ktn_ref, vp_ref, vc_ref, vn_ref, kct_ref, vctx_ref,
                 bias_ref, o_ref, kt_s, ktsh_s, v_s, qm_s, s_s, p_s, *, steps_per_seq):
    s = pl.program_id(2)
    step_tok = ROWS_PER_STEP * GRID_W
    win_tok = step_tok + 2 * Q_TOK
    n_groups = ROWS_PER_STEP // Q_ROWS
    lane = lax.broadcasted_iota(jnp.int32, (Q_TOK, LANES), 1)

    kt_s[:, 0:Q_TOK] = ktp_ref[0]
    kt_s[:, Q_TOK:Q_TOK + step_tok] = ktc_ref[0]
    kt_s[:, Q_TOK + step_tok:] = ktn_ref[0]
    ku = pltpu.bitcast(kt_s[...], jnp.uint32)
    ktsh_s[...] = pltpu.bitcast(pltpu.roll(ku, win_tok - GRID_W, axis=1), BF16)
    for a in range(HEADS_PER_GROUP):
        def own(vv):
            ln = lax.broadcasted_iota(jnp.int32, vv.shape, 1)
            keep = (ln >= a * HEAD_DIM) & (ln < (a + 1) * HEAD_DIM)
            return jnp.where(keep, vv, jnp.ones_like(vv))
        v_s[a, 0:Q_TOK, :] = own(vp_ref[0, 0])
        v_s[a, Q_TOK:Q_TOK + step_tok, :] = own(vc_ref[0, 0])
        v_s[a, Q_TOK + step_tok:win_tok, :] = own(vn_ref[0, 0])
        v_s[a, win_tok:, :] = own(vctx_ref[0, 0])
    for g in range(n_groups):
        qg = q_ref[0, 0, g * Q_TOK:(g + 1) * Q_TOK, :]
        for a in range(HEADS_PER_GROUP):
            in_head = (lane >= a * HEAD_DIM) & (lane < (a + 1) * HEAD_DIM)
            qm_s[g, a * Q_TOK:(a + 1) * Q_TOK, :] = jnp.where(in_head, qg, jnp.zeros_like(qg))

    def scores(g):
        for jj in range(Q_ROWS):
            j = g * Q_ROWS + jj
            lhs = jnp.concatenate([qm_s[g, a * Q_TOK + jj * GRID_W:a * Q_TOK + (jj + 1) * GRID_W, :]
                                   for a in range(HEADS_PER_GROUP)], axis=0)
            if j % 2 == 0:
                band = kt_s[:, j * GRID_W:j * GRID_W + BAND]
            else:
                band = ktsh_s[:, (j - 1) * GRID_W:(j - 1) * GRID_W + BAND]
            r = _dot(lhs, band)
            for a in range(HEADS_PER_GROUP):
                s_s[g % SCORE_SLOTS, a * Q_TOK + jj * GRID_W:a * Q_TOK + (jj + 1) * GRID_W, 0:BAND] = (
                    r[a * GRID_W:(a + 1) * GRID_W])
        s_s[g % SCORE_SLOTS, :, BAND:] = _dot(qm_s[g], kct_ref[0])

    def softmax(g):
        for a in range(HEADS_PER_GROUP):
            for jj in range(Q_ROWS):
                j = g * Q_ROWS + jj
                if j < Q_ROWS:
                    kind = jnp.where(s == 0, j, Q_ROWS)
                elif j >= ROWS_PER_STEP - Q_ROWS:
                    kind = jnp.where(s == steps_per_seq - 1, j - ROWS_PER_STEP + 2 * Q_ROWS + 1, Q_ROWS)
                else:
                    kind = Q_ROWS
                for t0 in range(0, GRID_W, STRIP):
                    r0 = a * Q_TOK + jj * GRID_W + t0
                    sl = s_s[g % SCORE_SLOTS, r0:r0 + STRIP, 0:BAND] + bias_ref[kind, 0, a, t0:t0 + STRIP, :]
                    sx = s_s[g % SCORE_SLOTS, r0:r0 + STRIP, BAND:]
                    m = jnp.maximum(jnp.max(sl, axis=-1, keepdims=True),
                                    jnp.max(sx, axis=-1, keepdims=True))
                    p_s[g % SCORE_SLOTS, r0:r0 + STRIP, 0:BAND] = jnp.exp2(sl - m).astype(BF16)
                    p_s[g % SCORE_SLOTS, r0:r0 + STRIP, BAND:] = jnp.exp2(sx - m).astype(BF16)

    def values(g):
        outs = []
        for a in range(HEADS_PER_GROUP):
            acc = _dot(p_s[g % SCORE_SLOTS, a * Q_TOK:(a + 1) * Q_TOK, BAND:], v_s[a, win_tok:, :])
            lat = []
            for jj in range(Q_ROWS):
                j = g * Q_ROWS + jj
                r0 = a * Q_TOK + jj * GRID_W
                lat.append(_dot(p_s[g % SCORE_SLOTS, r0:r0 + GRID_W, 0:BAND], v_s[a, j * GRID_W:j * GRID_W + BAND, :]))
            acc = acc + jnp.concatenate(lat, axis=0)
            outs.append(acc / pltpu.roll(acc, HEAD_DIM, axis=1))
        o = jnp.where(lane < HEAD_DIM, outs[0], outs[1])
        o_ref[0, 0, g * Q_TOK:(g + 1) * Q_TOK, :] = o.astype(BF16)

    scores(0)
    for g in range(n_groups):
        if g + 1 < n_groups:
            scores(g + 1)
        softmax(g)
        values(g)


def _attn_call(q, kt, v, kct, vctx, bias):
    n_seq, n_pairs, seq_len, _ = q.shape
    ctx_len = kct.shape[-1]
    step_tok = ROWS_PER_STEP * GRID_W
    win_tok = step_tok + 2 * Q_TOK
    steps = seq_len // step_tok
    n_tiles = seq_len // Q_TOK
    n_groups = ROWS_PER_STEP // Q_ROWS
    prev_t = lambda s: jnp.where(s == 0, 1, s * n_groups - 1)
    next_t = lambda s: jnp.where(s == steps - 1, n_tiles - 2, (s + 1) * n_groups)
    return pl.pallas_call(
        functools.partial(_attn_kernel, steps_per_seq=steps),
        grid=(n_pairs, n_seq, steps),
        in_specs=[
            pl.BlockSpec((1, 1, step_tok, LANES), lambda p, b, s: (b, p, s, 0)),
            pl.BlockSpec((1, LANES, Q_TOK), lambda p, b, s: (b, p, prev_t(s))),
            pl.BlockSpec((1, LANES, step_tok), lambda p, b, s: (b, p, s)),
            pl.BlockSpec((1, LANES, Q_TOK), lambda p, b, s: (b, p, next_t(s))),
            pl.BlockSpec((1, 1, Q_TOK, LANES), lambda p, b, s: (b, p, prev_t(s), 0)),
            pl.BlockSpec((1, 1, step_tok, LANES), lambda p, b, s: (b, p, s, 0)),
            pl.BlockSpec((1, 1, Q_TOK, LANES), lambda p, b, s: (b, p, next_t(s), 0)),
            pl.BlockSpec((1, LANES, ctx_len), lambda p, b, s: (b, p, 0)),
            pl.BlockSpec((1, 1, ctx_len, LANES), lambda p, b, s: (b, p, 0, 0)),
            pl.BlockSpec((2 * Q_ROWS + 1, 1, HEADS_PER_GROUP, GRID_W, BAND),
                         lambda p, b, s: (0, p, 0, 0, 0)),
        ],
        out_specs=pl.BlockSpec((1, 1, step_tok, LANES), lambda p, b, s: (b, p, s, 0)),
        out_shape=jax.ShapeDtypeStruct(q.shape, BF16),
        scratch_shapes=[
            pltpu.VMEM((LANES, win_tok), BF16),
            pltpu.VMEM((LANES, win_tok), BF16),
            pltpu.VMEM((HEADS_PER_GROUP, win_tok + ctx_len, LANES), BF16),
            pltpu.VMEM((n_groups, HEADS_PER_GROUP * Q_TOK, LANES), BF16),
            pltpu.VMEM((SCORE_SLOTS, HEADS_PER_GROUP * Q_TOK, BAND + ctx_len), F32),
            pltpu.VMEM((SCORE_SLOTS, HEADS_PER_GROUP * Q_TOK, BAND + ctx_len), BF16),
        ],
        compiler_params=_params(3),
        name="na_attn",
    )(q, kt, kt, kt, v, v, v, kct, vctx, bias)


def _out_kernel(o_ref, z_ref, x_ref, mod_ref, gpost_ref, w_ref, out_ref):
    tm, d = x_ref.shape
    n_groups = d // LANES
    o = jnp.concatenate([o_ref[0, p] for p in range(n_groups)], axis=-1).astype(F32)
    yv = (_silu(z_ref[...].astype(F32)) * o).astype(BF16)
    y = _dot(yv, w_ref[...])
    gt = mod_ref[0][:, 2 * d:]
    out_ref[...] = x_ref[...] + gt * (_rms(y) * gpost_ref[...])


def _out_call(o, z, x2, mod, g_post, w_out, *, seq_len, tm):
    n_tok, d = x2.shape
    tps = seq_len // tm
    n_seq = n_tok // seq_len
    n_groups = d // LANES
    const = lambda b, t: (0, 0)
    return pl.pallas_call(
        _out_kernel,
        grid=(n_seq, tps),
        in_specs=[
            pl.BlockSpec((1, n_groups, tm, LANES), lambda b, t: (b, 0, t, 0)),
            pl.BlockSpec((tm, d), lambda b, t: (b * tps + t, 0)),
            pl.BlockSpec((tm, d), lambda b, t: (b * tps + t, 0)),
            pl.BlockSpec((1, 1, 3 * d), lambda b, t: (b, 0, 0)),
            pl.BlockSpec((1, d), const),
            _resident((d, d), const),
        ],
        out_specs=pl.BlockSpec((tm, d), lambda b, t: (b * tps + t, 0)),
        out_shape=jax.ShapeDtypeStruct((n_tok, d), F32),
        compiler_params=_params(2),
        name="na_out",
    )(o, z, x2, mod, g_post, w_out)


def kernel(x, c, ctx, c_ctx, g_pre, g_post, w_mod, b_mod, w_in_conv, conv_w, w_out_conv,
           w_in_na, rpb, w_out_na):
    bsz, seq_len, d = x.shape
    _, ctx_len, _ = ctx.shape
    rows = seq_len // GRID_W
    assert g_pre.shape[0] == 2 and d == N_HEADS * HEAD_DIM
    assert rows % ROWS_PER_STEP == 0 and ROWS_PER_STEP >= 2 * Q_ROWS and rows >= 3 * Q_ROWS

    mod_rows = -(-(bsz + 1) // HALO) * HALO
    cvec = jnp.zeros((mod_rows, d), F32).at[:bsz].set(c).at[bsz].set(c_ctx)
    mods = _mod_call(cvec, w_mod, b_mod)
    mod_lat = [mods[i, :bsz].reshape(bsz, 1, 3 * d) for i in range(2)]
    mod_ctx = [mods[i, bsz:bsz + 1].reshape(1, 1, 3 * d) for i in range(2)]

    gp = [g_pre[i].reshape(1, d) for i in range(2)]
    gq = [g_post[i].reshape(1, d) for i in range(2)]

    w_in0 = w_in_conv[0].astype(BF16)
    w_out0 = w_out_conv[0].astype(BF16)
    wi = w_in_na[0]
    wq = wi[:, :d].astype(BF16)
    wkt = wi[:, d:2 * d].T.astype(BF16)
    wv = wi[:, 2 * d:3 * d].astype(BF16)
    wz = wi[:, 3 * d:].astype(BF16)
    weights = (gp[0], gq[0], gp[1], w_in0, conv_w[0], w_out0, wq, wkt, wv, wz)
    x1, q, kt, v, z = _front_call(x.reshape(bsz * seq_len, d), mod_lat[0], mod_lat[1], *weights,
                                  n_seq=bsz, seq_len=seq_len, tm=512, mod_per_seq=True)
    _, _, kct, vctx, _ = _front_call(ctx.reshape(bsz * ctx_len, d), mod_ctx[0], mod_ctx[1], *weights,
                                     n_seq=bsz, seq_len=ctx_len, tm=ctx_len, mod_per_seq=False)
    o = _attn_call(q, kt, v, kct, vctx, _bias_tables(rpb[0]))
    out = _out_call(o, z, x1, mod_lat[1], gq[1], w_out_na[0].astype(BF16), seq_len=seq_len, tm=1024)
    return out.reshape(bsz, seq_len, d)
```

```python
import functools

import jax
import jax.numpy as jnp
import numpy as np
from jax import lax
from jax.experimental import pallas as pl
from jax.experimental.pallas import tpu as pltpu

F32 = jnp.float32
BF16 = jnp.bfloat16

GRID_W = 64
N_HEADS = 16
HEAD_DIM = 64
WIN_R = 8
WIN_C = 16
EPS = 1e-6
NEG = -1e30

LANES = 128
HEADS_PER_GROUP = LANES // HEAD_DIM
HALO = 8
Q_ROWS = 4
Q_TOK = Q_ROWS * GRID_W
BAND = WIN_R * GRID_W
ROWS_PER_STEP = 64
SCORE_SLOTS = 3
OUT_CHUNKS = 4
STRIP = 16
LOG2E = 1.4426950408889634
VMEM_LIMIT = 56 * 1024 * 1024


def _resident(shape, index_map):
    return pl.BlockSpec(shape, index_map, pipeline_mode=pl.Buffered(1))


def _params(n_axes):
    return pltpu.CompilerParams(dimension_semantics=("arbitrary",) * n_axes,
                                vmem_limit_bytes=VMEM_LIMIT)


def _silu(v):
    return v * jax.nn.sigmoid(v)


def _rms(v):
    return v * lax.rsqrt(jnp.mean(v * v, axis=-1, keepdims=True) + EPS)


def _dot(a, b):
    return jnp.dot(a, b, preferred_element_type=F32)


def _mod_kernel(c_ref, w_ref, b_ref, o_ref):
    s = _silu(c_ref[...])
    o_ref[0] = jnp.dot(s, w_ref[0], preferred_element_type=F32,
                       precision=lax.Precision.HIGHEST) + b_ref[0]


def _mod_call(cvec, w_mod, b_mod):
    depth, d, d3 = w_mod.shape
    rows = cvec.shape[0]
    tn = 1024
    return pl.pallas_call(
        _mod_kernel,
        grid=(depth, d3 // tn),
        in_specs=[
            pl.BlockSpec((rows, d), lambda i, n: (0, 0)),
            pl.BlockSpec((1, d, tn), lambda i, n: (i, 0, n)),
            pl.BlockSpec((1, 1, tn), lambda i, n: (i, 0, n)),
        ],
        out_specs=pl.BlockSpec((1, rows, tn), lambda i, n: (i, 0, n)),
        out_shape=jax.ShapeDtypeStruct((depth, rows, d3), F32),
        compiler_params=_params(2),
        name="mod",
    )(cvec, w_mod, b_mod.reshape(depth, 1, d3))


def _norm_mod(xv, g, mod, d):
    sh = mod[:, :d]
    sc = mod[:, d:2 * d]
    return _rms(xv) * g * (1.0 + sc) + sh


def _front_kernel(x_ref, xp_ref, xn_ref, mod0_ref, mod1_ref, gpre0_ref, gpost0_ref, gpre1_ref,
                  win_ref, cw_ref, wout_ref, wq_ref, wkt_ref, wv_ref,
                  x1_ref, q_ref, kt_ref, v_ref, yv_ref, *, cb):
    tm, d = x_ref.shape
    n_pairs = d // LANES
    ti = pl.program_id(1)
    mod0 = mod0_ref[0]
    g0 = gpre0_ref[...]
    hm = _norm_mod(x_ref[...], g0, mod0, d)
    keep_p = (ti > 0).astype(F32)
    keep_n = (ti < pl.num_programs(1) - 1).astype(F32)
    hp = _norm_mod(xp_ref[...], g0, mod0, d)
    hn = _norm_mod(xn_ref[...], g0, mod0, d)
    hcat = jnp.concatenate([hp, hm, hn], axis=0).astype(BF16)
    hmb = hm.astype(BF16)
    rid = lax.broadcasted_iota(jnp.int32, (tm + 2 * HALO, 1), 0)
    rowkeep = jnp.where(rid < HALO, keep_p, jnp.where(rid >= tm + HALO, keep_n, 1.0))
    for nb in range(d // cb):
        c0 = nb * cb
        cg = _dot(hcat, win_ref[:, d + c0:d + c0 + cb])
        u = _dot(hcat, win_ref[:, 2 * d + c0:2 * d + c0 + cb])
        cu = cg * u * rowkeep
        w = cw_ref[:, c0:c0 + cb]
        conv = (w[0:1] * cu[HALO - 1:HALO - 1 + tm]
                + w[1:2] * cu[HALO:HALO + tm]
                + w[2:3] * cu[HALO + 1:HALO + 1 + tm])
        bg = _dot(hmb, win_ref[:, c0:c0 + cb])
        z0 = _dot(hmb, win_ref[:, 3 * d + c0:3 * d + c0 + cb])
        yv_ref[:, c0:c0 + cb] = (bg * conv * _silu(z0)).astype(BF16)
    half = tm // 2
    halves = (0, half)
    ys = [_dot(yv_ref[r0:r0 + half, :], wout_ref[...]) for r0 in halves]
    gt0 = mod0[:, 2 * d:]
    mod1 = mod1_ref[0]
    for r0, y in zip(halves, ys):
        x1 = x_ref[r0:r0 + half, :] + gt0 * (_rms(y) * gpost0_ref[...])
        x1_ref[r0:r0 + half, :] = x1
        hb = _norm_mod(x1, gpre1_ref[...], mod1, d).astype(BF16)
        q = _dot(hb, wq_ref[...]) * (HEAD_DIM ** -0.5 * LOG2E)
        for p in range(n_pairs):
            q_ref[0, p, r0:r0 + half, :] = q[:, p * LANES:(p + 1) * LANES].astype(BF16)
        v = _dot(hb, wv_ref[...])
        for p in range(n_pairs):
            v_ref[0, p, r0:r0 + half, :] = v[:, p * LANES:(p + 1) * LANES].astype(BF16)
        kt = lax.dot_general(wkt_ref[...], hb, (((1,), (1,)), ((), ())), preferred_element_type=F32)
        kt_ref[0, :, r0:r0 + half] = kt.astype(BF16)


def _front_call(x2, mod0, mod1, g_pre0, g_post0, g_pre1, w_in, conv_w, w_out, wq, wkt, wv,
                *, n_seq, seq_len, tm, mod_per_seq):
    n_tok, d = x2.shape
    tps = seq_len // tm
    n_pairs = d // LANES
    hb = tm // HALO
    last_hb = n_tok // HALO - 1
    tile = lambda b, t: b * tps + t
    if mod_per_seq:
        mod_map = lambda b, t: (b, 0, 0)
    else:
        mod_map = lambda b, t: (0, 0, 0)
    const = lambda b, t: (0, 0)
    paired = jax.ShapeDtypeStruct((n_seq, n_pairs, seq_len, LANES), BF16)
    return pl.pallas_call(
        functools.partial(_front_kernel, cb=256),
        grid=(n_seq, tps),
        in_specs=[
            pl.BlockSpec((tm, d), lambda b, t: (tile(b, t), 0)),
            pl.BlockSpec((HALO, d), lambda b, t: (jnp.maximum(tile(b, t) * hb - 1, 0), 0)),
            pl.BlockSpec((HALO, d), lambda b, t: (jnp.minimum((tile(b, t) + 1) * hb, last_hb), 0)),
            pl.BlockSpec((1, 1, 3 * d), mod_map),
            pl.BlockSpec((1, 1, 3 * d), mod_map),
            pl.BlockSpec((1, d), const),
            pl.BlockSpec((1, d), const),
            pl.BlockSpec((1, d), const),
            _resident((d, 4 * d), const),
            pl.BlockSpec((3, d), const),
            _resident((d, d), const),
            _resident((d, d), const),
            _resident((d, d), const),
            _resident((d, d), const),
        ],
        out_specs=[
            pl.BlockSpec((tm, d), lambda b, t: (tile(b, t), 0)),
            pl.BlockSpec((1, n_pairs, tm, LANES), lambda b, t: (b, 0, t, 0)),
            pl.BlockSpec((1, d, tm), lambda b, t: (b, 0, t)),
            pl.BlockSpec((1, n_pairs, tm, LANES), lambda b, t: (b, 0, t, 0)),
        ],
        out_shape=[
            jax.ShapeDtypeStruct((n_tok, d), F32),
            paired,
            jax.ShapeDtypeStruct((n_seq, d, seq_len), BF16),
            paired,
        ],
        scratch_shapes=[pltpu.VMEM((tm, d), BF16)],
        compiler_params=_params(2),
        name="front",
    )(x2, x2, x2, mod0, mod1, g_pre0, g_post0, g_pre1, w_in, conv_w, w_out, wq, wkt, wv)


def _band_rows():
    kinds = []
    for t in range(2 * Q_ROWS + 1):
        drs = []
        for i in range(WIN_R):
            if t < Q_ROWS:
                p = t + i
                key_minus_query = (p + Q_ROWS if p < Q_ROWS else p - Q_ROWS) - t
            elif t == Q_ROWS:
                key_minus_query = i - Q_ROWS
            else:
                jj = t - Q_ROWS - 1
                key_minus_query = i - Q_ROWS if i < WIN_R - jj else i - Q_ROWS - WIN_R
            drs.append(key_minus_query + WIN_R - 1)
        kinds.append(drs)
    return kinds


def _bias_kernel(rp_ref, o_ref):
    qc = lax.broadcasted_iota(jnp.int32, (GRID_W, GRID_W), 0)
    kc = lax.broadcasted_iota(jnp.int32, (GRID_W, GRID_W), 1)
    cs = jnp.clip(qc - WIN_C // 2, 0, GRID_W - WIN_C)
    col_ok = (kc >= cs) & (kc < cs + WIN_C)
    blocks = []
    for dr in range(2 * WIN_R - 1):
        row = jnp.broadcast_to(rp_ref[0, dr:dr + 1, :], (GRID_W, LANES))
        toep = pltpu.roll(row, GRID_W, axis=1, stride=1, stride_axis=0)[:, :GRID_W]
        blocks.append(jnp.where(col_ok, toep * LOG2E, NEG))
    for t, drs in enumerate(_band_rows()):
        o_ref[t, 0, 0] = jnp.concatenate([blocks[dr] for dr in drs], axis=1)


def _bias_tables(rpb):
    h, n_dr, n_dc = rpb.shape
    lane0 = GRID_W - WIN_C + 1
    rp = jnp.zeros((h, 2 * HALO, LANES), F32).at[:, :n_dr, lane0:lane0 + n_dc].set(rpb)
    n_kinds = 2 * Q_ROWS + 1
    return pl.pallas_call(
        _bias_kernel,
        grid=(h,),
        in_specs=[pl.BlockSpec((1, 2 * HALO, LANES), lambda i: (i, 0, 0))],
        out_specs=pl.BlockSpec((n_kinds, 1, 1, GRID_W, BAND),
                               lambda i: (0, i // HEADS_PER_GROUP, i % HEADS_PER_GROUP, 0, 0)),
        out_shape=jax.ShapeDtypeStruct((n_kinds, h // HEADS_PER_GROUP, HEADS_PER_GROUP, GRID_W, BAND), F32),
        compiler_params=_params(1),
        name="bias_tables",
    )(rp)


def _attn_kernel(q_ref, ktp_ref, ktc_ref, ktn_ref, vp_ref, vc_ref, vn_ref, kct_ref, vctx_ref,
                 bias_ref, o_ref, kt_s, ktsh_s, v_s, qm_s, s_s, p_s, *, steps_per_seq):
    s = pl.program_id(2)
    step_tok = ROWS_PER_STEP * GRID_W
    win_tok = step_tok + 2 * Q_TOK
    n_groups = ROWS_PER_STEP // Q_ROWS
    lane = lax.broadcasted_iota(jnp.int32, (Q_TOK, LANES), 1)

    kt_s[:, 0:Q_TOK] = ktp_ref[0]
    kt_s[:, Q_TOK:Q_TOK + step_tok] = ktc_ref[0]
    kt_s[:, Q_TOK + step_tok:] = ktn_ref[0]
    ku = pltpu.bitcast(kt_s[...], jnp.uint32)
    ktsh_s[...] = pltpu.bitcast(pltpu.roll(ku, win_tok - GRID_W, axis=1), BF16)
    for a in range(HEADS_PER_GROUP):
        def own(vv):
            ln = lax.broadcasted_iota(jnp.int32, vv.shape, 1)
            keep = (ln >= a * HEAD_DIM) & (ln < (a + 1) * HEAD_DIM)
            return jnp.where(keep, vv, jnp.ones_like(vv))
        v_s[a, 0:Q_TOK, :] = own(vp_ref[0, 0])
        v_s[a, Q_TOK:Q_TOK + step_tok, :] = own(vc_ref[0, 0])
        v_s[a, Q_TOK + step_tok:win_tok, :] = own(vn_ref[0, 0])
        v_s[a, win_tok:, :] = own(vctx_ref[0, 0])
    for g in range(n_groups):
        qg = q_ref[0, 0, g * Q_TOK:(g + 1) * Q_TOK, :]
        for a in range(HEADS_PER_GROUP):
            in_head = (lane >= a * HEAD_DIM) & (lane < (a + 1) * HEAD_DIM)
            qm_s[g, a * Q_TOK:(a + 1) * Q_TOK, :] = jnp.where(in_head, qg, jnp.zeros_like(qg))

    def scores(g):
        for jj in range(Q_ROWS):
            j = g * Q_ROWS + jj
            lhs = jnp.concatenate([qm_s[g, a * Q_TOK + jj * GRID_W:a * Q_TOK + (jj + 1) * GRID_W, :]
                                   for a in range(HEADS_PER_GROUP)], axis=0)
            if j % 2 == 0:
                band = kt_s[:, j * GRID_W:j * GRID_W + BAND]
            else:
                band = ktsh_s[:, (j - 1) * GRID_W:(j - 1) * GRID_W + BAND]
            r = _dot(lhs, band)
            for a in range(HEADS_PER_GROUP):
                s_s[g % SCORE_SLOTS, a * Q_TOK + jj * GRID_W:a * Q_TOK + (jj + 1) * GRID_W, 0:BAND] = (
                    r[a * GRID_W:(a + 1) * GRID_W])
        s_s[g % SCORE_SLOTS, :, BAND:] = _dot(qm_s[g], kct_ref[0])

    def softmax(g):
        for a in range(HEADS_PER_GROUP):
            for jj in range(Q_ROWS):
                j = g * Q_ROWS + jj
                if j < Q_ROWS:
                    kind = jnp.where(s == 0, j, Q_ROWS)
                elif j >= ROWS_PER_STEP - Q_ROWS:
                    kind = jnp.where(s == steps_per_seq - 1, j - ROWS_PER_STEP + 2 * Q_ROWS + 1, Q_ROWS)
                else:
                    kind = Q_ROWS
                for t0 in range(0, GRID_W, STRIP):
                    r0 = a * Q_TOK + jj * GRID_W + t0
                    sl = s_s[g % SCORE_SLOTS, r0:r0 + STRIP, 0:BAND] + bias_ref[kind, 0, a, t0:t0 + STRIP, :]
                    sx = s_s[g % SCORE_SLOTS, r0:r0 + STRIP, BAND:]
                    m = jnp.maximum(jnp.max(sl, axis=-1, keepdims=True),
                                    jnp.max(sx, axis=-1, keepdims=True))
                    p_s[g % SCORE_SLOTS, r0:r0 + STRIP, 0:BAND] = jnp.exp2(sl - m).astype(BF16)
                    p_s[g % SCORE_SLOTS, r0:r0 + STRIP, BAND:] = jnp.exp2(sx - m).astype(BF16)

    def values(g):
        outs = []
        for a in range(HEADS_PER_GROUP):
            acc = _dot(p_s[g % SCORE_SLOTS, a * Q_TOK:(a + 1) * Q_TOK, BAND:], v_s[a, win_tok:, :])
            lat = []
            for jj in range(Q_ROWS):
                j = g * Q_ROWS + jj
                r0 = a * Q_TOK + jj * GRID_W
                lat.append(_dot(p_s[g % SCORE_SLOTS, r0:r0 + GRID_W, 0:BAND], v_s[a, j * GRID_W:j * GRID_W + BAND, :]))
            acc = acc + jnp.concatenate(lat, axis=0)
            outs.append(acc / pltpu.roll(acc, HEAD_DIM, axis=1))
        o = jnp.where(lane < HEAD_DIM, outs[0], outs[1])
        o_ref[0, 0, g * Q_TOK:(g + 1) * Q_TOK, :] = o.astype(BF16)

    scores(0)
    for g in range(n_groups):
        if g + 1 < n_groups:
            scores(g + 1)
        softmax(g)
        values(g)


def _attn_call(q, kt, v, kct, vctx, bias):
    n_seq, n_pairs, seq_len, _ = q.shape
    ctx_len = kct.shape[-1]
    step_tok = ROWS_PER_STEP * GRID_W
    win_tok = step_tok + 2 * Q_TOK
    steps = seq_len // step_tok
    n_tiles = seq_len // Q_TOK
    n_groups = ROWS_PER_STEP // Q_ROWS
    prev_t = lambda s: jnp.where(s == 0, 1, s * n_groups - 1)
    next_t = lambda s: jnp.where(s == steps - 1, n_tiles - 2, (s + 1) * n_groups)
    return pl.pallas_call(
        functools.partial(_attn_kernel, steps_per_seq=steps),
        grid=(n_pairs, n_seq, steps),
        in_specs=[
            pl.BlockSpec((1, 1, step_tok, LANES), lambda p, b, s: (b, p, s, 0)),
            pl.BlockSpec((1, LANES, Q_TOK), lambda p, b, s: (b, p, prev_t(s))),
            pl.BlockSpec((1, LANES, step_tok), lambda p, b, s: (b, p, s)),
            pl.BlockSpec((1, LANES, Q_TOK), lambda p, b, s: (b, p, next_t(s))),
            pl.BlockSpec((1, 1, Q_TOK, LANES), lambda p, b, s: (b, p, prev_t(s), 0)),
            pl.BlockSpec((1, 1, step_tok, LANES), lambda p, b, s: (b, p, s, 0)),
            pl.BlockSpec((1, 1, Q_TOK, LANES), lambda p, b, s: (b, p, next_t(s), 0)),
            pl.BlockSpec((1, LANES, ctx_len), lambda p, b, s: (b, p, 0)),
            pl.BlockSpec((1, 1, ctx_len, LANES), lambda p, b, s: (b, p, 0, 0)),
            pl.BlockSpec((2 * Q_ROWS + 1, 1, HEADS_PER_GROUP, GRID_W, BAND),
                         lambda p, b, s: (0, p, 0, 0, 0)),
        ],
        out_specs=pl.BlockSpec((1, 1, step_tok, LANES), lambda p, b, s: (b, p, s, 0)),
        out_shape=jax.ShapeDtypeStruct(q.shape, BF16),
        scratch_shapes=[
            pltpu.VMEM((LANES, win_tok), BF16),
            pltpu.VMEM((LANES, win_tok), BF16),
            pltpu.VMEM((HEADS_PER_GROUP, win_tok + ctx_len, LANES), BF16),
            pltpu.VMEM((n_groups, HEADS_PER_GROUP * Q_TOK, LANES), BF16),
            pltpu.VMEM((SCORE_SLOTS, HEADS_PER_GROUP * Q_TOK, BAND + ctx_len), F32),
            pltpu.VMEM((SCORE_SLOTS, HEADS_PER_GROUP * Q_TOK, BAND + ctx_len), BF16),
        ],
        compiler_params=_params(3),
        name="na_attn",
    )(q, kt, kt, kt, v, v, v, kct, vctx, bias)


def _out_kernel(o_ref, x_ref, mod_ref, gpre_ref, gpost_ref, wz_ref, w_ref, out_ref):
    tm, d = x_ref.shape
    n_pairs = d // LANES
    mod = mod_ref[0]
    gt = mod[:, 2 * d:]
    chunk = tm // OUT_CHUNKS
    starts = range(0, tm, chunk)
    zs = [_dot(_norm_mod(x_ref[r0:r0 + chunk, :], gpre_ref[...], mod, d).astype(BF16), wz_ref[...])
          for r0 in starts]
    for r0, z in zip(starts, zs):
        o = jnp.concatenate([o_ref[0, p, r0:r0 + chunk, :] for p in range(n_pairs)], axis=-1)
        yv = (_silu(z) * o.astype(F32)).astype(BF16)
        y = _dot(yv, w_ref[...])
        out_ref[r0:r0 + chunk, :] = x_ref[r0:r0 + chunk, :] + gt * (_rms(y) * gpost_ref[...])


def _out_call(o, x2, mod, g_pre, g_post, wz, w_out, *, seq_len, tm):
    n_tok, d = x2.shape
    tps = seq_len // tm
    n_seq = n_tok // seq_len
    n_pairs = d // LANES
    const = lambda b, t: (0, 0)
    return pl.pallas_call(
        _out_kernel,
        grid=(n_seq, tps),
        in_specs=[
            pl.BlockSpec((1, n_pairs, tm, LANES), lambda b, t: (b, 0, t, 0)),
            pl.BlockSpec((tm, d), lambda b, t: (b * tps + t, 0)),
            pl.BlockSpec((1, 1, 3 * d), lambda b, t: (b, 0, 0)),
            pl.BlockSpec((1, d), const),
            pl.BlockSpec((1, d), const),
            _resident((d, d), const),
            _resident((d, d), const),
        ],
        out_specs=pl.BlockSpec((tm, d), lambda b, t: (b * tps + t, 0)),
        out_shape=jax.ShapeDtypeStruct((n_tok, d), F32),
        compiler_params=_params(2),
        name="na_out",
    )(o, x2, mod, g_pre, g_post, wz, w_out)


def kernel(x, c, ctx, c_ctx, g_pre, g_post, w_mod, b_mod, w_in_conv, conv_w, w_out_conv,
           w_in_na, rpb, w_out_na):
    bsz, seq_len, d = x.shape
    _, ctx_len, _ = ctx.shape
    rows = seq_len // GRID_W
    assert g_pre.shape[0] == 2 and d == N_HEADS * HEAD_DIM
    assert rows % ROWS_PER_STEP == 0 and ROWS_PER_STEP >= 2 * Q_ROWS and rows >= 3 * Q_ROWS

    mod_rows = -(-(bsz + 1) // HALO) * HALO
    cvec = jnp.zeros((mod_rows, d), F32).at[:bsz].set(c).at[bsz].set(c_ctx)
    mods = _mod_call(cvec, w_mod, b_mod)
    mod_lat = [mods[i, :bsz].reshape(bsz, 1, 3 * d) for i in range(2)]
    mod_ctx = [mods[i, bsz:bsz + 1].reshape(1, 1, 3 * d) for i in range(2)]

    gp = [g_pre[i].reshape(1, d) for i in range(2)]
    gq = [g_post[i].reshape(1, d) for i in range(2)]

    w_in0 = w_in_conv[0].astype(BF16)
    w_out0 = w_out_conv[0].astype(BF16)
    wi = w_in_na[0]
    wq = wi[:, :d].astype(BF16)
    wkt = wi[:, d:2 * d].T.astype(BF16)
    wv = wi[:, 2 * d:3 * d].astype(BF16)
    wz = wi[:, 3 * d:].astype(BF16)
    weights = (gp[0], gq[0], gp[1], w_in0, conv_w[0], w_out0, wq, wkt, wv)
    x1, q, kt, v = _front_call(x.reshape(bsz * seq_len, d), mod_lat[0], mod_lat[1], *weights,
                                  n_seq=bsz, seq_len=seq_len, tm=512, mod_per_seq=True)
    _, _, kct, vctx = _front_call(ctx.reshape(bsz * ctx_len, d), mod_ctx[0], mod_ctx[1], *weights,
                                     n_seq=bsz, seq_len=ctx_len, tm=ctx_len, mod_per_seq=False)
    o = _attn_call(q, kt, v, kct, vctx, _bias_tables(rpb[0]))
    out = _out_call(o, x1, mod_lat[1], gp[1], gq[1], wz, w_out_na[0].astype(BF16), seq_len=seq_len, tm=1024)
    return out.reshape(bsz, seq_len, d)
```

```python
import functools

import jax
import jax.numpy as jnp
import numpy as np
from jax import lax
from jax.experimental import pallas as pl
from jax.experimental.pallas import tpu as pltpu

F32 = jnp.float32
BF16 = jnp.bfloat16

GRID_W = 64
N_HEADS = 16
HEAD_DIM = 64
WIN_R = 8
WIN_C = 16
EPS = 1e-6
NEG = -1e30

LANES = 128
HEADS_PER_GROUP = LANES // HEAD_DIM
HALO = 8
Q_ROWS = 4
Q_TOK = Q_ROWS * GRID_W
BAND = WIN_R * GRID_W
ROWS_PER_STEP = 64
SCORE_SLOTS = 3
OUT_CHUNKS = 4
STRIP = 16
LOG2E = 1.4426950408889634
VMEM_LIMIT = 56 * 1024 * 1024


def _resident(shape, index_map):
    return pl.BlockSpec(shape, index_map, pipeline_mode=pl.Buffered(1))


def _params(n_axes):
    return pltpu.CompilerParams(dimension_semantics=("arbitrary",) * n_axes,
                                vmem_limit_bytes=VMEM_LIMIT)


def _silu(v):
    return v * jax.nn.sigmoid(v)


def _rms(v):
    return v * lax.rsqrt(jnp.mean(v * v, axis=-1, keepdims=True) + EPS)


def _dot(a, b):
    return jnp.dot(a, b, preferred_element_type=F32)


def _mod_kernel(c_ref, w_ref, b_ref, o_ref):
    s = _silu(c_ref[...])
    o_ref[0] = jnp.dot(s, w_ref[0], preferred_element_type=F32,
                       precision=lax.Precision.HIGHEST) + b_ref[0]


def _mod_call(cvec, w_mod, b_mod):
    depth, d, d3 = w_mod.shape
    rows = cvec.shape[0]
    tn = 1024
    return pl.pallas_call(
        _mod_kernel,
        grid=(depth, d3 // tn),
        in_specs=[
            pl.BlockSpec((rows, d), lambda i, n: (0, 0)),
            pl.BlockSpec((1, d, tn), lambda i, n: (i, 0, n)),
            pl.BlockSpec((1, 1, tn), lambda i, n: (i, 0, n)),
        ],
        out_specs=pl.BlockSpec((1, rows, tn), lambda i, n: (i, 0, n)),
        out_shape=jax.ShapeDtypeStruct((depth, rows, d3), F32),
        compiler_params=_params(2),
        name="mod",
    )(cvec, w_mod, b_mod.reshape(depth, 1, d3))


def _norm_mod(xv, g, mod, d):
    sh = mod[:, :d]
    sc = mod[:, d:2 * d]
    return _rms(xv) * g * (1.0 + sc) + sh


def _front_kernel(x_ref, xp_ref, xn_ref, mod0_ref, mod1_ref, gpre0_ref, gpost0_ref, gpre1_ref,
                  win_ref, cw_ref, wout_ref, wq_ref, wkt_ref, wv_ref,
                  x1_ref, q_ref, kt_ref, v_ref, yv_ref, *, cb):
    tm, d = x_ref.shape
    n_pairs = d // LANES
    ti = pl.program_id(1)
    mod0 = mod0_ref[0]
    g0 = gpre0_ref[...]
    hm = _norm_mod(x_ref[...], g0, mod0, d)
    keep_p = (ti > 0).astype(F32)
    keep_n = (ti < pl.num_programs(1) - 1).astype(F32)
    hp = _norm_mod(xp_ref[...], g0, mod0, d)
    hn = _norm_mod(xn_ref[...], g0, mod0, d)
    hcat = jnp.concatenate([hp, hm, hn], axis=0).astype(BF16)
    hmb = hm.astype(BF16)
    rid = lax.broadcasted_iota(jnp.int32, (tm + 2 * HALO, 1), 0)
    rowkeep = jnp.where(rid < HALO, keep_p, jnp.where(rid >= tm + HALO, keep_n, 1.0))
    for nb in range(d // cb):
        c0 = nb * cb
        cg = _dot(hcat, win_ref[:, d + c0:d + c0 + cb])
        u = _dot(hcat, win_ref[:, 2 * d + c0:2 * d + c0 + cb])
        cu = cg * u * rowkeep
        w = cw_ref[:, c0:c0 + cb]
        conv = (w[0:1] * cu[HALO - 1:HALO - 1 + tm]
                + w[1:2] * cu[HALO:HALO + tm]
                + w[2:3] * cu[HALO + 1:HALO + 1 + tm])
        bg = _dot(hmb, win_ref[:, c0:c0 + cb])
        z0 = _dot(hmb, win_ref[:, 3 * d + c0:3 * d + c0 + cb])
        yv_ref[:, c0:c0 + cb] = (bg * conv * _silu(z0)).astype(BF16)
    half = tm // 2
    halves = (0, half)
    ys = [_dot(yv_ref[r0:r0 + half, :], wout_ref[...]) for r0 in halves]
    gt0 = mod0[:, 2 * d:]
    mod1 = mod1_ref[0]
    for r0, y in zip(halves, ys):
        x1 = x_ref[r0:r0 + half, :] + gt0 * (_rms(y) * gpost0_ref[...])
        x1_ref[r0:r0 + half, :] = x1
        hb = _norm_mod(x1, gpre1_ref[...], mod1, d).astype(BF16)
        q = _dot(hb, wq_ref[...]) * (HEAD_DIM ** -0.5 * LOG2E)
        for p in range(n_pairs):
            q_ref[0, p, r0:r0 + half, :] = q[:, p * LANES:(p + 1) * LANES].astype(BF16)
        v = _dot(hb, wv_ref[...])
        for p in range(n_pairs):
            v_ref[0, p, r0:r0 + half, :] = v[:, p * LANES:(p + 1) * LANES].astype(BF16)
        kt = lax.dot_general(wkt_ref[...], hb, (((1,), (1,)), ((), ())), preferred_element_type=F32)
        kt_ref[0, :, r0:r0 + half] = kt.astype(BF16)


def _front_call(x2, mod0, mod1, g_pre0, g_post0, g_pre1, w_in, conv_w, w_out, wq, wkt, wv,
                *, n_seq, seq_len, tm, mod_per_seq):
    n_tok, d = x2.shape
    tps = seq_len // tm
    n_pairs = d // LANES
    hb = tm // HALO
    last_hb = n_tok // HALO - 1
    tile = lambda b, t: b * tps + t
    if mod_per_seq:
        mod_map = lambda b, t: (b, 0, 0)
    else:
        mod_map = lambda b, t: (0, 0, 0)
    const = lambda b, t: (0, 0)
    paired = jax.ShapeDtypeStruct((n_seq, n_pairs, seq_len, LANES), BF16)
    return pl.pallas_call(
        functools.partial(_front_kernel, cb=256),
        grid=(n_seq, tps),
        in_specs=[
            pl.BlockSpec((tm, d), lambda b, t: (tile(b, t), 0)),
            pl.BlockSpec((HALO, d), lambda b, t: (jnp.maximum(tile(b, t) * hb - 1, 0), 0)),
            pl.BlockSpec((HALO, d), lambda b, t: (jnp.minimum((tile(b, t) + 1) * hb, last_hb), 0)),
            pl.BlockSpec((1, 1, 3 * d), mod_map),
            pl.BlockSpec((1, 1, 3 * d), mod_map),
            pl.BlockSpec((1, d), const),
            pl.BlockSpec((1, d), const),
            pl.BlockSpec((1, d), const),
            _resident((d, 4 * d), const),
            pl.BlockSpec((3, d), const),
            _resident((d, d), const),
            _resident((d, d), const),
            _resident((d, d), const),
            _resident((d, d), const),
        ],
        out_specs=[
            pl.BlockSpec((tm, d), lambda b, t: (tile(b, t), 0)),
            pl.BlockSpec((1, n_pairs, tm, LANES), lambda b, t: (b, 0, t, 0)),
            pl.BlockSpec((1, d, tm), lambda b, t: (b, 0, t)),
            pl.BlockSpec((1, n_pairs, tm, LANES), lambda b, t: (b, 0, t, 0)),
        ],
        out_shape=[
            jax.ShapeDtypeStruct((n_tok, d), F32),
            paired,
            jax.ShapeDtypeStruct((n_seq, d, seq_len), BF16),
            paired,
        ],
        scratch_shapes=[pltpu.VMEM((tm, d), BF16)],
        compiler_params=_params(2),
        name="front",
    )(x2, x2, x2, mod0, mod1, g_pre0, g_post0, g_pre1, w_in, conv_w, w_out, wq, wkt, wv)


def _band_rows():
    kinds = []
    for t in range(2 * Q_ROWS + 1):
        drs = []
        for i in range(WIN_R):
            if t < Q_ROWS:
                p = t + i
                key_minus_query = (p + Q_ROWS if p < Q_ROWS else p - Q_ROWS) - t
            elif t == Q_ROWS:
                key_minus_query = i - Q_ROWS
            else:
                jj = t - Q_ROWS - 1
                key_minus_query = i - Q_ROWS if i < WIN_R - jj else i - Q_ROWS - WIN_R
            drs.append(key_minus_query + WIN_R - 1)
        kinds.append(drs)
    return kinds


def _bias_kernel(rp_ref, o_ref):
    qc = lax.broadcasted_iota(jnp.int32, (GRID_W, GRID_W), 0)
    kc = lax.broadcasted_iota(jnp.int32, (GRID_W, GRID_W), 1)
    cs = jnp.clip(qc - WIN_C // 2, 0, GRID_W - WIN_C)
    col_ok = (kc >= cs) & (kc < cs + WIN_C)
    blocks = []
    for dr in range(2 * WIN_R - 1):
        row = jnp.broadcast_to(rp_ref[0, dr:dr + 1, :], (GRID_W, LANES))
        toep = pltpu.roll(row, GRID_W, axis=1, stride=1, stride_axis=0)[:, :GRID_W]
        blocks.append(jnp.where(col_ok, toep * LOG2E, NEG))
    for t, drs in enumerate(_band_rows()):
        o_ref[t, 0, 0] = jnp.concatenate([blocks[dr] for dr in drs], axis=1)


def _bias_tables(rpb):
    h, n_dr, n_dc = rpb.shape
    lane0 = GRID_W - WIN_C + 1
    rp = jnp.zeros((h, 2 * HALO, LANES), F32).at[:, :n_dr, lane0:lane0 + n_dc].set(rpb)
    n_kinds = 2 * Q_ROWS + 1
    return pl.pallas_call(
        _bias_kernel,
        grid=(h,),
        in_specs=[pl.BlockSpec((1, 2 * HALO, LANES), lambda i: (i, 0, 0))],
        out_specs=pl.BlockSpec((n_kinds, 1, 1, GRID_W, BAND),
                               lambda i: (0, i // HEADS_PER_GROUP, i % HEADS_PER_GROUP, 0, 0)),
        out_shape=jax.ShapeDtypeStruct((n_kinds, h // HEADS_PER_GROUP, HEADS_PER_GROUP, GRID_W, BAND), F32),
        compiler_params=_params(1),
        name="bias_tables",
    )(rp)


def _attn_kernel(q_ref, ktp_ref, ktc_ref, ktn_ref, vp_ref, vc_ref, vn_ref, kct_ref, vctx_ref,
                 bias_ref, o_ref, kt_s, ktsh_s, v_s, qm_s, s_s, p_s, *, steps_per_seq):
    s = pl.program_id(2)
    step_tok = ROWS_PER_STEP * GRID_W
    win_tok = step_tok + 2 * Q_TOK
    n_groups = ROWS_PER_STEP // Q_ROWS
    lane = lax.broadcasted_iota(jnp.int32, (Q_TOK, LANES), 1)

    kt_s[:, 0:Q_TOK] = ktp_ref[0]
    kt_s[:, Q_TOK:Q_TOK + step_tok] = ktc_ref[0]
    kt_s[:, Q_TOK + step_tok:] = ktn_ref[0]
    ku = pltpu.bitcast(kt_s[...], jnp.uint32)
    ktsh_s[...] = pltpu.bitcast(pltpu.roll(ku, win_tok - GRID_W, axis=1), BF16)
    for a in range(HEADS_PER_GROUP):
        def own(vv):
            ln = lax.broadcasted_iota(jnp.int32, vv.shape, 1)
            keep = (ln >= a * HEAD_DIM) & (ln < (a + 1) * HEAD_DIM)
            return jnp.where(keep, vv, jnp.ones_like(vv))
        cols = slice(a * LANES, (a + 1) * LANES)
        v_s[0:Q_TOK, cols] = own(vp_ref[0, 0])
        v_s[Q_TOK:Q_TOK + step_tok, cols] = own(vc_ref[0, 0])
        v_s[Q_TOK + step_tok:win_tok, cols] = own(vn_ref[0, 0])
        v_s[win_tok:, cols] = own(vctx_ref[0, 0])
    for g in range(n_groups):
        qg = q_ref[0, 0, g * Q_TOK:(g + 1) * Q_TOK, :]
        for a in range(HEADS_PER_GROUP):
            in_head = (lane >= a * HEAD_DIM) & (lane < (a + 1) * HEAD_DIM)
            qm_s[g, a * Q_TOK:(a + 1) * Q_TOK, :] = jnp.where(in_head, qg, jnp.zeros_like(qg))

    def scores(g):
        for jj in range(Q_ROWS):
            j = g * Q_ROWS + jj
            lhs = jnp.concatenate([qm_s[g, a * Q_TOK + jj * GRID_W:a * Q_TOK + (jj + 1) * GRID_W, :]
                                   for a in range(HEADS_PER_GROUP)], axis=0)
            if j % 2 == 0:
                band = kt_s[:, j * GRID_W:j * GRID_W + BAND]
            else:
                band = ktsh_s[:, (j - 1) * GRID_W:(j - 1) * GRID_W + BAND]
            r = _dot(lhs, band)
            for a in range(HEADS_PER_GROUP):
                s_s[g % SCORE_SLOTS, a * Q_TOK + jj * GRID_W:a * Q_TOK + (jj + 1) * GRID_W, 0:BAND] = (
                    r[a * GRID_W:(a + 1) * GRID_W])
        s_s[g % SCORE_SLOTS, :, BAND:] = _dot(qm_s[g], kct_ref[0])

    def softmax(g):
        for a in range(HEADS_PER_GROUP):
            for jj in range(Q_ROWS):
                j = g * Q_ROWS + jj
                if j < Q_ROWS:
                    kind = jnp.where(s == 0, j, Q_ROWS)
                elif j >= ROWS_PER_STEP - Q_ROWS:
                    kind = jnp.where(s == steps_per_seq - 1, j - ROWS_PER_STEP + 2 * Q_ROWS + 1, Q_ROWS)
                else:
                    kind = Q_ROWS
                for t0 in range(0, GRID_W, STRIP):
                    r0 = a * Q_TOK + jj * GRID_W + t0
                    sl = s_s[g % SCORE_SLOTS, r0:r0 + STRIP, 0:BAND] + bias_ref[kind, 0, a, t0:t0 + STRIP, :]
                    sx = s_s[g % SCORE_SLOTS, r0:r0 + STRIP, BAND:]
                    m = jnp.maximum(jnp.max(sl, axis=-1, keepdims=True),
                                    jnp.max(sx, axis=-1, keepdims=True))
                    p_s[g % SCORE_SLOTS, r0:r0 + STRIP, 0:BAND] = jnp.exp2(sl - m).astype(BF16)
                    p_s[g % SCORE_SLOTS, r0:r0 + STRIP, BAND:] = jnp.exp2(sx - m).astype(BF16)

    def values(g):
        k = g % SCORE_SLOTS
        ctx_acc = _dot(p_s[k, :, BAND:], v_s[win_tok:, :])
        lat = [[], []]
        for jj in range(Q_ROWS):
            j = g * Q_ROWS + jj
            lhs = jnp.concatenate([p_s[k, a * Q_TOK + jj * GRID_W:a * Q_TOK + (jj + 1) * GRID_W, 0:BAND]
                                   for a in range(HEADS_PER_GROUP)], axis=0)
            r = _dot(lhs, v_s[j * GRID_W:j * GRID_W + BAND, :])
            for a in range(HEADS_PER_GROUP):
                lat[a].append(r[a * GRID_W:(a + 1) * GRID_W, a * LANES:(a + 1) * LANES])
        outs = []
        for a in range(HEADS_PER_GROUP):
            acc = ctx_acc[a * Q_TOK:(a + 1) * Q_TOK, a * LANES:(a + 1) * LANES] + jnp.concatenate(lat[a], axis=0)
            outs.append(acc / pltpu.roll(acc, HEAD_DIM, axis=1))
        o = jnp.where(lane < HEAD_DIM, outs[0], outs[1])
        o_ref[0, 0, g * Q_TOK:(g + 1) * Q_TOK, :] = o.astype(BF16)

    scores(0)
    for g in range(n_groups):
        if g + 1 < n_groups:
            scores(g + 1)
        softmax(g)
        values(g)


def _attn_call(q, kt, v, kct, vctx, bias):
    n_seq, n_pairs, seq_len, _ = q.shape
    ctx_len = kct.shape[-1]
    step_tok = ROWS_PER_STEP * GRID_W
    win_tok = step_tok + 2 * Q_TOK
    steps = seq_len // step_tok
    n_tiles = seq_len // Q_TOK
    n_groups = ROWS_PER_STEP // Q_ROWS
    prev_t = lambda s: jnp.where(s == 0, 1, s * n_groups - 1)
    next_t = lambda s: jnp.where(s == steps - 1, n_tiles - 2, (s + 1) * n_groups)
    return pl.pallas_call(
        functools.partial(_attn_kernel, steps_per_seq=steps),
        grid=(n_pairs, n_seq, steps),
        in_specs=[
            pl.BlockSpec((1, 1, step_tok, LANES), lambda p, b, s: (b, p, s, 0)),
            pl.BlockSpec((1, LANES, Q_TOK), lambda p, b, s: (b, p, prev_t(s))),
            pl.BlockSpec((1, LANES, step_tok), lambda p, b, s: (b, p, s)),
            pl.BlockSpec((1, LANES, Q_TOK), lambda p, b, s: (b, p, next_t(s))),
            pl.BlockSpec((1, 1, Q_TOK, LANES), lambda p, b, s: (b, p, prev_t(s), 0)),
            pl.BlockSpec((1, 1, step_tok, LANES), lambda p, b, s: (b, p, s, 0)),
            pl.BlockSpec((1, 1, Q_TOK, LANES), lambda p, b, s: (b, p, next_t(s), 0)),
            pl.BlockSpec((1, LANES, ctx_len), lambda p, b, s: (b, p, 0)),
            pl.BlockSpec((1, 1, ctx_len, LANES), lambda p, b, s: (b, p, 0, 0)),
            pl.BlockSpec((2 * Q_ROWS + 1, 1, HEADS_PER_GROUP, GRID_W, BAND),
                         lambda p, b, s: (0, p, 0, 0, 0)),
        ],
        out_specs=pl.BlockSpec((1, 1, step_tok, LANES), lambda p, b, s: (b, p, s, 0)),
        out_shape=jax.ShapeDtypeStruct(q.shape, BF16),
        scratch_shapes=[
            pltpu.VMEM((LANES, win_tok), BF16),
            pltpu.VMEM((LANES, win_tok), BF16),
            pltpu.VMEM((win_tok + ctx_len, HEADS_PER_GROUP * LANES), BF16),
            pltpu.VMEM((n_groups, HEADS_PER_GROUP * Q_TOK, LANES), BF16),
            pltpu.VMEM((SCORE_SLOTS, HEADS_PER_GROUP * Q_TOK, BAND + ctx_len), F32),
            pltpu.VMEM((SCORE_SLOTS, HEADS_PER_GROUP * Q_TOK, BAND + ctx_len), BF16),
        ],
        compiler_params=_params(3),
        name="na_attn",
    )(q, kt, kt, kt, v, v, v, kct, vctx, bias)


def _out_kernel(o_ref, x_ref, mod_ref, gpre_ref, gpost_ref, wz_ref, w_ref, out_ref):
    tm, d = x_ref.shape
    n_pairs = d // LANES
    mod = mod_ref[0]
    gt = mod[:, 2 * d:]
    chunk = tm // OUT_CHUNKS
    starts = range(0, tm, chunk)
    zs = [_dot(_norm_mod(x_ref[r0:r0 + chunk, :], gpre_ref[...], mod, d).astype(BF16), wz_ref[...])
          for r0 in starts]
    for r0, z in zip(starts, zs):
        o = jnp.concatenate([o_ref[0, p, r0:r0 + chunk, :] for p in range(n_pairs)], axis=-1)
        yv = (_silu(z) * o.astype(F32)).astype(BF16)
        y = _dot(yv, w_ref[...])
        out_ref[r0:r0 + chunk, :] = x_ref[r0:r0 + chunk, :] + gt * (_rms(y) * gpost_ref[...])


def _out_call(o, x2, mod, g_pre, g_post, wz, w_out, *, seq_len, tm):
    n_tok, d = x2.shape
    tps = seq_len // tm
    n_seq = n_tok // seq_len
    n_pairs = d // LANES
    const = lambda b, t: (0, 0)
    return pl.pallas_call(
        _out_kernel,
        grid=(n_seq, tps),
        in_specs=[
            pl.BlockSpec((1, n_pairs, tm, LANES), lambda b, t: (b, 0, t, 0)),
            pl.BlockSpec((tm, d), lambda b, t: (b * tps + t, 0)),
            pl.BlockSpec((1, 1, 3 * d), lambda b, t: (b, 0, 0)),
            pl.BlockSpec((1, d), const),
            pl.BlockSpec((1, d), const),
            _resident((d, d), const),
            _resident((d, d), const),
        ],
        out_specs=pl.BlockSpec((tm, d), lambda b, t: (b * tps + t, 0)),
        out_shape=jax.ShapeDtypeStruct((n_tok, d), F32),
        compiler_params=_params(2),
        name="na_out",
    )(o, x2, mod, g_pre, g_post, wz, w_out)


def kernel(x, c, ctx, c_ctx, g_pre, g_post, w_mod, b_mod, w_in_conv, conv_w, w_out_conv,
           w_in_na, rpb, w_out_na):
    bsz, seq_len, d = x.shape
    _, ctx_len, _ = ctx.shape
    rows = seq_len // GRID_W
    assert g_pre.shape[0] == 2 and d == N_HEADS * HEAD_DIM
    assert rows % ROWS_PER_STEP == 0 and ROWS_PER_STEP >= 2 * Q_ROWS and rows >= 3 * Q_ROWS

    mod_rows = -(-(bsz + 1) // HALO) * HALO
    cvec = jnp.zeros((mod_rows, d), F32).at[:bsz].set(c).at[bsz].set(c_ctx)
    mods = _mod_call(cvec, w_mod, b_mod)
    mod_lat = [mods[i, :bsz].reshape(bsz, 1, 3 * d) for i in range(2)]
    mod_ctx = [mods[i, bsz:bsz + 1].reshape(1, 1, 3 * d) for i in range(2)]

    gp = [g_pre[i].reshape(1, d) for i in range(2)]
    gq = [g_post[i].reshape(1, d) for i in range(2)]

    w_in0 = w_in_conv[0].astype(BF16)
    w_out0 = w_out_conv[0].astype(BF16)
    wi = w_in_na[0]
    wq = wi[:, :d].astype(BF16)
    wkt = wi[:, d:2 * d].T.astype(BF16)
    wv = wi[:, 2 * d:3 * d].astype(BF16)
    wz = wi[:, 3 * d:].astype(BF16)
    weights = (gp[0], gq[0], gp[1], w_in0, conv_w[0], w_out0, wq, wkt, wv)
    x1, q, kt, v = _front_call(x.reshape(bsz * seq_len, d), mod_lat[0], mod_lat[1], *weights,
                                  n_seq=bsz, seq_len=seq_len, tm=512, mod_per_seq=True)
    _, _, kct, vctx = _front_call(ctx.reshape(bsz * ctx_len, d), mod_ctx[0], mod_ctx[1], *weights,
                                     n_seq=bsz, seq_len=ctx_len, tm=ctx_len, mod_per_seq=False)
    o = _attn_call(q, kt, v, kct, vctx, _bias_tables(rpb[0]))
    out = _out_call(o, x1, mod_lat[1], gp[1], gq[1], wz, w_out_na[0].astype(BF16), seq_len=seq_len, tm=1024)
    return out.reshape(bsz, seq_len, d)
```
